```python
import numpy as np
import jax
import jax.numpy as jnp
from jax import lax

D_MODEL = 1024
BATCH = 16
SEQ = 2048
DEPTH = 4

GRID_W = 64
CTX_LEN = 256
HEAD_DIM = 64
MIX_WIDTH = D_MODEL
ATT_WIDTH = MIX_WIDTH // 2
RET_WIDTH = MIX_WIDTH // 4
FOURIER_WIDTH = MIX_WIDTH - ATT_WIDTH - RET_WIDTH
ATT_Q_HEADS = ATT_WIDTH // HEAD_DIM
ATT_KV_HEADS = ATT_Q_HEADS // 4
ATT_GROUP = ATT_Q_HEADS // ATT_KV_HEADS
KV_WIDTH = ATT_KV_HEADS * HEAD_DIM
WINDOW = 128
BLOCK = 128
RET_HEADS = RET_WIDTH // HEAD_DIM
RET_CHUNK = 128
FOURIER_GROUPS = 4
FOURIER_DIM = FOURIER_WIDTH // FOURIER_GROUPS
PROJ_SIZES = (ATT_WIDTH, KV_WIDTH, KV_WIDTH, RET_WIDTH, RET_WIDTH, RET_WIDTH, RET_WIDTH, RET_WIDTH, FOURIER_WIDTH)
PROJ_WIDTH = ATT_WIDTH + 2 * KV_WIDTH + 5 * RET_WIDTH + FOURIER_WIDTH
D_FF = 4 * D_MODEL
N_MOD = 6
ROPE_BASE = 10000.0
EPS = 1e-6
NEG_INF = -1e30

kernel_name = 'hybrid_parallel_heads_dit_block'


def rms_norm(x, g):
    xf = x.astype(jnp.float32)
    y = xf * lax.rsqrt(jnp.mean(xf * xf, axis=-1, keepdims=True) + EPS)
    return (y * g.astype(jnp.float32)).astype(x.dtype)


def modulate(x, g, shift, scale):
    return rms_norm(x, g) * (1 + scale) + shift


def rope_table(pos, dim):
    half = dim // 2
    freqs = ROPE_BASE ** (-jnp.arange(half, dtype=jnp.float32) / half)
    ang = pos.astype(jnp.float32)[:, None] * freqs[None, :]
    return jnp.cos(ang)[:, None, :], jnp.sin(ang)[:, None, :]


def rope_rotate(x, cos, sin):
    cos = cos.astype(x.dtype)
    sin = sin.astype(x.dtype)
    x1, x2 = jnp.split(x, 2, axis=-1)
    return jnp.concatenate([x1 * cos - x2 * sin, x2 * cos + x1 * sin], axis=-1)


def axial_rope(x, row_tab, col_tab):
    xr, xc = jnp.split(x, 2, axis=-1)
    return jnp.concatenate([rope_rotate(xr, *row_tab), rope_rotate(xc, *col_tab)], axis=-1)


def split_heads(t, n_heads):
    return t.reshape(t.shape[0], t.shape[1], n_heads, -1)


def split_proj(p):
    offsets = np.cumsum(PROJ_SIZES)[:-1].tolist()
    return jnp.split(p, offsets, axis=-1)


def window_attention_latent(q, k, v, k_ctx, v_ctx, sink):
    b, s, _, d = q.shape
    nb = s // BLOCK
    lc = k_ctx.shape[1]
    qb = q.reshape(b, nb, BLOCK, ATT_KV_HEADS, ATT_GROUP, d)

    def band(t):
        tp = jnp.pad(t, ((0, 0), (BLOCK, BLOCK), (0, 0), (0, 0))).reshape(b, nb + 2, BLOCK, ATT_KV_HEADS, d)
        return jnp.concatenate([tp[:, :-2], tp[:, 1:-1], tp[:, 2:]], axis=2)

    kb, vb = band(k), band(v)
    scale = d ** -0.5
    s_band = jnp.einsum('bnqhgd,bnkhd->bnhgqk', qb, kb).astype(jnp.float32) * scale
    s_ctx = jnp.einsum('bnqhgd,bkhd->bnhgqk', qb, k_ctx).astype(jnp.float32) * scale
    qi = jnp.arange(BLOCK)
    kj = jnp.arange(3 * BLOCK)
    kpos = jnp.arange(nb)[:, None] * BLOCK - BLOCK + kj[None, :]
    in_window = jnp.abs(kj[None, :] - BLOCK - qi[:, None]) <= WINDOW
    valid = in_window[None] & ((kpos >= 0) & (kpos < s))[:, None, :]
    s_band = jnp.where(valid[None, :, None, None], s_band, NEG_INF)
    s_sink = jnp.broadcast_to(
        sink.astype(jnp.float32).reshape(1, 1, ATT_KV_HEADS, ATT_GROUP, 1, 1), s_band.shape[:-1] + (1,))
    p = jax.nn.softmax(jnp.concatenate([s_band, s_ctx, s_sink], axis=-1), axis=-1)
    p_band = p[..., :3 * BLOCK].astype(v.dtype)
    p_ctx = p[..., 3 * BLOCK:3 * BLOCK + lc].astype(v.dtype)
    o = (jnp.einsum('bnhgqk,bnkhd->bnqhgd', p_band, vb)
         + jnp.einsum('bnhgqk,bkhd->bnqhgd', p_ctx, v_ctx))
    return o.reshape(b, s, ATT_Q_HEADS * d)


def context_attention(q, k, v, sink):
    b, l, _, d = q.shape
    qg = q.reshape(b, l, ATT_KV_HEADS, ATT_GROUP, d)
    s = jnp.einsum('bqhgd,bkhd->bhgqk', qg, k).astype(jnp.float32) * d ** -0.5
    s_sink = jnp.broadcast_to(
        sink.astype(jnp.float32).reshape(1, ATT_KV_HEADS, ATT_GROUP, 1, 1), s.shape[:-1] + (1,))
    p = jax.nn.softmax(jnp.concatenate([s, s_sink], axis=-1), axis=-1)
    o = jnp.einsum('bhgqk,bkhd->bqhgd', p[..., :l].astype(v.dtype), v)
    return o.reshape(b, l, ATT_Q_HEADS * d)


def retention_scan(q, k, v, log_gamma, state0):
    b, l, h, _ = q.shape
    n = l // RET_CHUNK
    idx = jnp.arange(RET_CHUNK, dtype=jnp.float32)
    diff = idx[:, None] - idx[None, :]
    inner_decay = jnp.where(diff[None] >= 0,
                            jnp.exp(jnp.maximum(diff, 0.0)[None] * log_gamma[:, None, None]), 0.0)
    xi = jnp.exp((idx[:, None] + 1.0) * log_gamma[None, :])
    zeta = jnp.exp((RET_CHUNK - 1.0 - idx)[:, None] * log_gamma[None, :])
    chunk_decay = jnp.exp(RET_CHUNK * log_gamma)

    def to_chunks(t):
        return t.reshape(b, n, RET_CHUNK, h, t.shape[-1]).swapaxes(0, 1)

    def step(state, inp):
        qc, kc, vc = inp
        sc = jnp.einsum('bihd,bjhd->bhij', qc, kc) * inner_decay[None]
        inner = jnp.einsum('bhij,bjhe->bihe', sc, vc)
        cross = jnp.einsum('bihd,bhde->bihe', qc, state) * xi[None, :, :, None]
        new_state = (state * chunk_decay[None, :, None, None]
                     + jnp.einsum('bjhd,bjhe->bhde', kc * zeta[None, :, :, None], vc))
        return new_state, inner + cross

    final, out = lax.scan(step, state0, (to_chunks(q), to_chunks(k), to_chunks(v)))
    return out.swapaxes(0, 1).reshape(b, l, h, v.shape[-1]), final


def head_group_norm(o, g):
    mu = jnp.mean(o, axis=-1, keepdims=True)
    var = jnp.mean(jnp.square(o - mu), axis=-1, keepdims=True)
    y = (o - mu) * lax.rsqrt(var + EPS)
    return y.reshape(o.shape[0], o.shape[1], -1) * g.astype(jnp.float32)


def fourier_mix(u, w):
    b, l, _ = u.shape
    uf = u.astype(jnp.float32).reshape(b, l, FOURIER_GROUPS, FOURIER_DIM)
    y = jnp.fft.fft2(uf, axes=(1, 3), norm='ortho').real
    return jnp.einsum('blgc,gcd->blgd', y, w.astype(jnp.float32)).reshape(b, l, FOURIER_WIDTH).astype(u.dtype)


def sq_relu_mlp(h, w1, w2):
    return jnp.square(jax.nn.relu(h @ w1)) @ w2


def token_mixers(a_lat, a_ctx, w_in, w_out, q_g, k_g, sink, decay_logit, gn_g, four_w,
                 row_tab, col_tab, t_tab, need_ctx):
    aq, ak, av, rq, rk, rv, rgf, rgb, fu = split_proj(a_lat @ w_in)
    cq, ck, cv, crq, crk, crv, crgf, crgb, cfu = split_proj(a_ctx @ w_in)

    q = axial_rope(rms_norm(split_heads(aq, ATT_Q_HEADS), q_g), row_tab, col_tab)
    k = axial_rope(rms_norm(split_heads(ak, ATT_KV_HEADS), k_g), row_tab, col_tab)
    v = split_heads(av, ATT_KV_HEADS)
    kc = rms_norm(split_heads(ck, ATT_KV_HEADS), k_g)
    vc = split_heads(cv, ATT_KV_HEADS)
    att_lat = window_attention_latent(q, k, v, kc, vc, sink)

    log_gamma = -jax.nn.softplus(-decay_logit.astype(jnp.float32))

    def ret_qkv(tq, tk, tv):
        return (split_heads(tq, RET_HEADS).astype(jnp.float32),
                split_heads(tk, RET_HEADS).astype(jnp.float32) * HEAD_DIM ** -0.5,
                split_heads(tv, RET_HEADS).astype(jnp.float32))

    lq, lk, lv = ret_qkv(rq, rk, rv)
    lq = rope_rotate(lq, *t_tab)
    lk = rope_rotate(lk, *t_tab)
    xq, xk, xv = ret_qkv(crq, crk, crv)
    zero = jnp.zeros((a_lat.shape[0], RET_HEADS, HEAD_DIM, HEAD_DIM), jnp.float32)

    def flip(t):
        return jnp.flip(t, axis=1)

    oc_f, st_f = retention_scan(xq, xk, xv, log_gamma[0], zero)
    oc_b, st_b = retention_scan(flip(xq), flip(xk), flip(xv), log_gamma[1], zero)
    ol_f, _ = retention_scan(lq, lk, lv, log_gamma[0], st_f)
    ol_b, _ = retention_scan(flip(lq), flip(lk), flip(lv), log_gamma[1], st_b)

    def ret_out(o_f, o_b, gf, gb):
        y = (head_group_norm(o_f, gn_g) * jax.nn.silu(gf.astype(jnp.float32))
             + head_group_norm(o_b, gn_g) * jax.nn.silu(gb.astype(jnp.float32)))
        return y.astype(gf.dtype)

    ret_lat = ret_out(ol_f, flip(ol_b), rgf, rgb)

    four_lat = fourier_mix(fu, four_w)

    m_lat = jnp.concatenate([att_lat, ret_lat, four_lat], axis=-1) @ w_out
    if not need_ctx:
        return m_lat, None

    qc = rms_norm(split_heads(cq, ATT_Q_HEADS), q_g)
    att_ctx = context_attention(qc, kc, vc, sink)
    ret_ctx = ret_out(oc_f, flip(oc_b), crgf, crgb)
    four_ctx = fourier_mix(cfu, four_w)
    m_ctx = jnp.concatenate([att_ctx, ret_ctx, four_ctx], axis=-1) @ w_out
    return m_lat, m_ctx


def setup_inputs(seed: int = 0) -> dict:
    key = jax.random.key(seed)
    ks = jax.random.split(key, 20)
    f32 = jnp.float32

    def nrm(k, shape, scale):
        return jax.random.normal(k, shape, f32) * scale

    base_logit = jnp.asarray(np.log(2.0 ** (5 + np.arange(RET_HEADS)) - 1.0), f32)
    return {
        'x': nrm(ks[0], (BATCH, SEQ, D_MODEL), 1.0),
        'c': nrm(ks[1], (BATCH, D_MODEL), 1.0),
        'ctx': nrm(ks[2], (BATCH, CTX_LEN, D_MODEL), 1.0),
        'c_ctx': nrm(ks[3], (D_MODEL,), 1.0),
        'w_mod': nrm(ks[4], (DEPTH, D_MODEL, N_MOD * D_MODEL), 0.5 * D_MODEL ** -0.5),
        'b_mod': nrm(ks[5], (DEPTH, N_MOD * D_MODEL), 0.01),
        'norm1_g': 1.0 + nrm(ks[6], (DEPTH, D_MODEL), 0.01),
        'norm2_g': 1.0 + nrm(ks[7], (DEPTH, D_MODEL), 0.01),
        'w_in': nrm(ks[8], (DEPTH, D_MODEL, PROJ_WIDTH), D_MODEL ** -0.5),
        'w_out': nrm(ks[9], (DEPTH, MIX_WIDTH, D_MODEL), MIX_WIDTH ** -0.5),
        'q_norm_g': 1.0 + nrm(ks[10], (DEPTH, HEAD_DIM), 0.01),
        'k_norm_g': 1.0 + nrm(ks[11], (DEPTH, HEAD_DIM), 0.01),
        'attn_sink': nrm(ks[12], (DEPTH, ATT_Q_HEADS), 0.5),
        'ret_decay_logit': base_logit[None, None, :] + nrm(ks[13], (DEPTH, 2, RET_HEADS), 0.1),
        'ret_gn_g': 1.0 + nrm(ks[14], (DEPTH, RET_WIDTH), 0.01),
        'fourier_w': nrm(ks[15], (DEPTH, FOURIER_GROUPS, FOURIER_DIM, FOURIER_DIM), FOURIER_DIM ** -0.5),
        'w_ff1': nrm(ks[16], (DEPTH, D_MODEL, D_FF), D_MODEL ** -0.5),
        'w_ff2': nrm(ks[17], (DEPTH, D_FF, D_MODEL), D_FF ** -0.5),
    }


def reference(x, c, ctx, c_ctx, w_mod, b_mod, norm1_g, norm2_g, w_in, w_out, q_norm_g, k_norm_g,
              attn_sink, ret_decay_logit, ret_gn_g, fourier_w, w_ff1, w_ff2):
    s = x.shape[1]
    rows = s // GRID_W
    rr, cc = jnp.meshgrid(jnp.arange(rows), jnp.arange(GRID_W), indexing='ij')
    row_tab = rope_table(rr.reshape(-1), HEAD_DIM // 2)
    col_tab = rope_table(cc.reshape(-1), HEAD_DIM // 2)
    t_tab = rope_table(jnp.arange(s), HEAD_DIM)
    silu_c = jax.nn.silu(c)
    silu_cc = jax.nn.silu(c_ctx)
    h = ctx
    for l in range(DEPTH):
        need_ctx = l < DEPTH - 1
        mod_lat = jnp.split((silu_c @ w_mod[l] + b_mod[l])[:, None, :], N_MOD, axis=-1)
        mod_ctx = jnp.split((silu_cc @ w_mod[l] + b_mod[l])[None, None, :], N_MOD, axis=-1)
        a_lat = modulate(x, norm1_g[l], mod_lat[0], mod_lat[1])
        a_ctx = modulate(h, norm1_g[l], mod_ctx[0], mod_ctx[1])
        m_lat, m_ctx = token_mixers(a_lat, a_ctx, w_in[l], w_out[l], q_norm_g[l], k_norm_g[l], attn_sink[l],
                                    ret_decay_logit[l], ret_gn_g[l], fourier_w[l],
                                    row_tab, col_tab, t_tab, need_ctx)
        x = x + mod_lat[2] * m_lat
        x = x + mod_lat[5] * sq_relu_mlp(modulate(x, norm2_g[l], mod_lat[3], mod_lat[4]), w_ff1[l], w_ff2[l])
        if need_ctx:
            h = h + mod_ctx[2] * m_ctx
            h = h + mod_ctx[5] * sq_relu_mlp(modulate(h, norm2_g[l], mod_ctx[3], mod_ctx[4]), w_ff1[l], w_ff2[l])
    return x
```

```python
import functools

import numpy as np
import jax
import jax.numpy as jnp
from jax import lax
from jax.experimental import pallas as pl
from jax.experimental.pallas import tpu as pltpu

F32 = jnp.float32
BF16 = jnp.bfloat16

HEAD_DIM = 64
LANES = 128
GRID_W = 64
N_MOD = 6
ATT_Q_HEADS = 8
ATT_KV_HEADS = 2
WINDOW = 128
BLOCK = 128
RET_HEADS = 4
RET_CHUNK = 128
FOURIER_GROUPS = 4
ROPE_BASE = 10000.0
EPS = 1e-6
NEG_INF = -1e30
VMEM_LIMIT_BYTES = 56 * 1024 * 1024

ATT_W = ATT_Q_HEADS * HEAD_DIM
KV_W = ATT_KV_HEADS * HEAD_DIM
RET_W = RET_HEADS * HEAD_DIM
FOUR_W = 256
O_Q, O_K, O_V = 0, ATT_W, ATT_W + KV_W
O_RQ = ATT_W + 2 * KV_W
O_RK, O_RV, O_GF, O_GB = O_RQ + RET_W, O_RQ + 2 * RET_W, O_RQ + 3 * RET_W, O_RQ + 4 * RET_W
O_FU = O_RQ + 5 * RET_W
PROJ_W = O_FU + FOUR_W


def _cparams(n_axes):
    return pltpu.CompilerParams(dimension_semantics=("arbitrary",) * n_axes,
                                vmem_limit_bytes=VMEM_LIMIT_BYTES)


def _silu(v):
    return v / (1.0 + jnp.exp(-v))


def _mod_kernel(c_ref, w_ref, b_ref, o_ref):
    s = _silu(c_ref[...]).astype(BF16)
    o_ref[...] = jnp.dot(s, w_ref[...].astype(BF16), preferred_element_type=F32) + b_ref[...]


def _mod_call(cvec, w_mod, b_mod):
    depth, d, n = w_mod.shape
    r = cvec.shape[0]
    tn = 1536
    return pl.pallas_call(
        _mod_kernel,
        grid=(depth, n // tn),
        in_specs=[pl.BlockSpec((r, d), lambda l, j: (0, 0)),
                  pl.BlockSpec((None, d, tn), lambda l, j: (l, 0, j)),
                  pl.BlockSpec((None, 1, tn), lambda l, j: (l, 0, j))],
        out_specs=pl.BlockSpec((None, r, tn), lambda l, j: (l, 0, j)),
        out_shape=jax.ShapeDtypeStruct((depth, r, n), F32),
        compiler_params=_cparams(2),
        name="mod_vectors",
    )(cvec, w_mod, b_mod.reshape(depth, 1, n))


def _fourier_prep_kernel(w_ref, c_ref, s_ref, wc_ref, ws_ref):
    w = w_ref[...]
    wc_ref[...] = jnp.dot(c_ref[...], w, preferred_element_type=F32, precision=lax.Precision.HIGHEST)
    ws_ref[...] = jnp.dot(s_ref[...], w, preferred_element_type=F32, precision=lax.Precision.HIGHEST)


def _fourier_prep_call(w_bd, c_bd, s_bd):
    depth, n, _ = w_bd.shape
    spec_w = pl.BlockSpec((None, n, n), lambda l: (l, 0, 0))
    spec_t = pl.BlockSpec((n, n), lambda l: (0, 0))
    return pl.pallas_call(
        _fourier_prep_kernel,
        grid=(depth,),
        in_specs=[spec_w, spec_t, spec_t],
        out_specs=[spec_w, spec_w],
        out_shape=[jax.ShapeDtypeStruct((depth, n, n), F32)] * 2,
        compiler_params=_cparams(1),
        name="fourier_prep",
    )(w_bd, c_bd, s_bd)


def _rope(v, cos, sin_next, sin_prev, shift):
    return (v * cos + pltpu.roll(v, LANES - shift, 1) * sin_next + pltpu.roll(v, shift, 1) * sin_prev)


def _even_odd_variants(v):
    lo = lax.broadcasted_iota(jnp.int32, v.shape, 1) < HEAD_DIM
    vr = pltpu.roll(v, HEAD_DIM, 1)
    zero = jnp.zeros_like(v)
    return (jnp.where(lo, v, zero), jnp.where(lo, zero, vr), jnp.where(lo, vr, zero), jnp.where(lo, zero, v))


def _inproj_kernel(x_ref, mod_ref, g_ref, w_ref, gq_ref, gk_ref, hsum_ref, wcs_ref,
                   ca_ref, san_ref, sap_ref, ct_ref, stn_ref, stp_ref,
                   q_ref, kz_ref, vz_ref, rq_ref, rk_ref, rv_ref, gate_ref, u_ref, *, rope):
    x = x_ref[...]
    shift = mod_ref[0:1, :]
    scale = mod_ref[1:2, :]
    ms = jnp.mean(x * x, axis=-1, keepdims=True)
    a = ((x * lax.rsqrt(ms + EPS)) * g_ref[...]) * (1.0 + scale) + shift
    ab = a.astype(BF16)

    def proj(lo, hi):
        return jnp.dot(ab, w_ref[:, lo:hi], preferred_element_type=F32)

    hsum = hsum_ref[...]
    inv_d = 1.0 / HEAD_DIM

    def head_norm(v, gain):
        ssq = jnp.dot((v * v).astype(BF16), hsum, preferred_element_type=F32)
        return v * lax.rsqrt(ssq * inv_d + EPS) * gain

    q = proj(O_Q, O_K)
    gq = gq_ref[...]
    for j in range(ATT_W // 256):
        qn = head_norm(q[:, j * 256:(j + 1) * 256], gq)
        for c in range(2):
            col = qn[:, c * LANES:(c + 1) * LANES]
            if rope:
                col = _rope(col, ca_ref[...], san_ref[...], sap_ref[...], HEAD_DIM // 4)
            lo = j * 256 + c * LANES
            q_ref[:, lo:lo + LANES] = (col * HEAD_DIM ** -0.5).astype(BF16)

    k = proj(O_K, O_V)
    ssq = jnp.dot((k * k).astype(BF16), hsum[0:LANES, 0:LANES], preferred_element_type=F32)
    k = k * lax.rsqrt(ssq * inv_d + EPS) * gk_ref[...]
    if rope:
        k = _rope(k, ca_ref[...], san_ref[...], sap_ref[...], HEAD_DIM // 4)
    for i, var in enumerate(_even_odd_variants(k)):
        kz_ref[:, i * LANES:(i + 1) * LANES] = var.astype(BF16)
    v = proj(O_V, O_RQ)
    for i, var in enumerate(_even_odd_variants(v)):
        vz_ref[:, i * LANES:(i + 1) * LANES] = var.astype(BF16)

    rq = proj(O_RQ, O_RK)
    rk = proj(O_RK, O_RV) * HEAD_DIM ** -0.5
    for c in range(RET_W // LANES):
        qc = rq[:, c * LANES:(c + 1) * LANES]
        kc = rk[:, c * LANES:(c + 1) * LANES]
        if rope:
            qc = _rope(qc, ct_ref[...], stn_ref[...], stp_ref[...], HEAD_DIM // 2)
            kc = _rope(kc, ct_ref[...], stn_ref[...], stp_ref[...], HEAD_DIM // 2)
        rq_ref[:, c * LANES:(c + 1) * LANES] = qc.astype(BF16)
        rk_ref[:, c * LANES:(c + 1) * LANES] = kc.astype(BF16)
    rv_ref[...] = proj(O_RV, O_GF).astype(BF16)
    gate_ref[...] = proj(O_GF, O_FU).astype(BF16)

    fu = proj(O_FU, PROJ_W).astype(BF16)
    u_ref[...] = jnp.dot(fu, wcs_ref[...], preferred_element_type=F32).astype(BF16)


def _inproj_call(x, mod, mod_row, g1, w_in, gq, gk, hsum, wcs, tabs, *, rope, tm):
    b, t, d = x.shape
    row = lambda i, j: (0, 0)
    tab = pl.BlockSpec((tm, LANES), lambda i, j: (j, 0))
    tok = lambda w: pl.BlockSpec((None, tm, w), lambda i, j: (i, j, 0))
    out_w = (ATT_W, 4 * LANES, 4 * LANES, RET_W, RET_W, RET_W, 2 * RET_W, 2 * FOUR_W)
    return pl.pallas_call(
        functools.partial(_inproj_kernel, rope=rope),
        grid=(b, t // tm),
        in_specs=[tok(d),
                  pl.BlockSpec((None, N_MOD, d), lambda i, j: (mod_row(i), 0, 0)),
                  pl.BlockSpec((1, d), row),
                  pl.BlockSpec((d, PROJ_W), row),
                  pl.BlockSpec((1, 256), row),
                  pl.BlockSpec((1, LANES), row),
                  pl.BlockSpec((256, 256), row),
                  pl.BlockSpec((FOUR_W, 2 * FOUR_W), row),
                  tab, tab, tab, tab, tab, tab],
        out_specs=[tok(w) for w in out_w],
        out_shape=[jax.ShapeDtypeStruct((b, t, w), BF16) for w in out_w],
        compiler_params=_cparams(2),
        name="in_projection",
    )(x, mod, g1, w_in, gq, gk, hsum, wcs, *tabs)


def _attn_kernel(sink_ref, q_ref, kz_ref, vz_ref, kcz_ref, vcz_ref, o_ref, *, band, t):
    n = pl.program_id(1)
    contract_last = (((1,), (1,)), ((), ()))
    if band:
        span = 3 * BLOCK
        start = pl.multiple_of(jnp.clip((n - 1) * BLOCK, 0, t - span), BLOCK)
        qpos = n * BLOCK + lax.broadcasted_iota(jnp.int32, (2 * BLOCK, span), 0) % BLOCK
        kpos = start + lax.broadcasted_iota(jnp.int32, (2 * BLOCK, span), 1)
        valid = jnp.abs(kpos - qpos) <= WINDOW
    upper = lax.broadcasted_iota(jnp.int32, (2 * BLOCK, 1), 0) < BLOCK
    lane_lo = lax.broadcasted_iota(jnp.int32, (2 * BLOCK, LANES), 1) < HEAD_DIM
    for h in range(ATT_KV_HEADS):
        qq = jnp.concatenate([q_ref[:, (2 * h) * LANES:(2 * h + 1) * LANES],
                              q_ref[:, (2 * h + 1) * LANES:(2 * h + 2) * LANES]], axis=0)
        acc = jnp.zeros((2 * BLOCK, LANES), F32)
        inv = []
        for par in range(2):
            col = (2 * h + par) * LANES
            s_ctx = lax.dot_general(qq, kcz_ref[:, col:col + LANES], contract_last, preferred_element_type=F32)
            sink = jnp.where(upper, sink_ref[4 * h + par], sink_ref[4 * h + 2 + par])
            m = jnp.maximum(jnp.max(s_ctx, axis=-1, keepdims=True), sink)
            if band:
                s_band = lax.dot_general(qq, kz_ref[pl.ds(start, span), col:col + LANES], contract_last,
                                         preferred_element_type=F32)
                s_band = jnp.where(valid, s_band, NEG_INF)
                m = jnp.maximum(m, jnp.max(s_band, axis=-1, keepdims=True))
            p_ctx = jnp.exp(s_ctx - m)
            den = jnp.sum(p_ctx, axis=-1, keepdims=True) + jnp.exp(sink - m)
            acc = acc + jnp.dot(p_ctx.astype(BF16), vcz_ref[:, col:col + LANES], preferred_element_type=F32)
            if band:
                p_band = jnp.exp(s_band - m)
                den = den + jnp.sum(p_band, axis=-1, keepdims=True)
                acc = acc + jnp.dot(p_band.astype(BF16), vz_ref[pl.ds(start, span), col:col + LANES],
                                    preferred_element_type=F32)
            inv.append(1.0 / den)
        out = acc * jnp.where(lane_lo, inv[0], inv[1])
        o_ref[:, (2 * h) * LANES:(2 * h + 1) * LANES] = out[0:BLOCK].astype(BF16)
        o_ref[:, (2 * h + 1) * LANES:(2 * h + 2) * LANES] = out[BLOCK:2 * BLOCK].astype(BF16)


def _attn_call(sink, q, kz, vz, kcz, vcz, *, band):
    b, t, _ = q.shape
    lc = kcz.shape[1]
    full = lambda rows: pl.BlockSpec((None, rows, 4 * LANES), lambda i, j: (i, 0, 0))
    blk = pl.BlockSpec((None, BLOCK, ATT_W), lambda i, j: (i, j, 0))
    return pl.pallas_call(
        functools.partial(_attn_kernel, band=band, t=t),
        grid=(b, t // BLOCK),
        in_specs=[pl.BlockSpec(memory_space=pltpu.SMEM), blk, full(kz.shape[1]), full(vz.shape[1]), full(lc), full(lc)],
        out_specs=blk,
        out_shape=jax.ShapeDtypeStruct((b, t, ATT_W), BF16),
        compiler_params=_cparams(2),
        name="window_attention" if band else "context_attention",
    )(sink, q, kz, vz, kcz, vcz)


def _ret_kernel(lg_ref, lgl_ref, gn_ref, havg_ref,
                rq_ref, rk_ref, rv_ref, gate_ref, cq_ref, ck_ref, cv_ref, cgate_ref,
                o_ref, oc_ref,
                dec_ref, st_ref, of_ref, ob_ref, ocf_ref, ocb_ref, *, t, lc):
    c = RET_CHUNK
    ri = lax.broadcasted_iota(jnp.int32, (c, c), 0)
    ci = lax.broadcasted_iota(jnp.int32, (c, c), 1)
    for h in range(RET_HEADS):
        diff = (ri - ci).astype(F32)
        dec_ref[0, h] = jnp.where(ri >= ci, jnp.exp(jnp.maximum(diff, 0.0) * lg_ref[0, h]), 0.0)
        dec_ref[1, h] = jnp.where(ci >= ri, jnp.exp(jnp.maximum(-diff, 0.0) * lg_ref[1, h]), 0.0)
    idx = lax.broadcasted_iota(jnp.int32, (c, RET_W), 0).astype(F32)
    lgf = lgl_ref[0:1, :]
    lgb = lgl_ref[1:2, :]
    xi = (jnp.exp((idx + 1.0) * lgf), jnp.exp((c - idx) * lgb))
    zeta = (jnp.exp((c - 1.0 - idx) * lgf), jnp.exp(idx * lgb))
    cdec = (jnp.exp(c * lgf), jnp.exp(c * lgb))
    lane_head = lax.broadcasted_iota(jnp.int32, (1, RET_W), 1) // HEAD_DIM
    head_mask = [lane_head == h for h in range(RET_HEADS)]
    blockdiag = (lax.broadcasted_iota(jnp.int32, (RET_W, RET_W), 0) // HEAD_DIM
                 == lax.broadcasted_iota(jnp.int32, (RET_W, RET_W), 1) // HEAD_DIM)
    contract_last = (((1,), (1,)), ((), ()))
    contract_first = (((0,), (0,)), ((), ()))

    def chunk(direction, q, k, v):
        zq = jnp.zeros_like(q)
        probs = []
        for h in range(RET_HEADS):
            s = lax.dot_general(jnp.where(head_mask[h], q, zq), k, contract_last, preferred_element_type=F32)
            probs.append((s * dec_ref[direction, h]).astype(BF16))
        p = jnp.concatenate(probs, axis=1)
        vb = jnp.concatenate([jnp.where(head_mask[h], v, jnp.zeros_like(v)) for h in range(RET_HEADS)], axis=0)
        inner = jnp.dot(p, vb, preferred_element_type=F32)
        st = st_ref[direction]
        cross = jnp.dot(q, st.astype(BF16), preferred_element_type=F32) * xi[direction]
        kd = (k.astype(F32) * zeta[direction]).astype(BF16)
        kv = lax.dot_general(kd, v, contract_first, preferred_element_type=F32)
        st_ref[direction] = st * cdec[direction] + jnp.where(blockdiag, kv, 0.0)
        return inner + cross

    st_ref[...] = jnp.zeros_like(st_ref)

    def run(n_chunks, q_ref_, k_ref_, v_ref_, outf_ref, outb_ref):
        def body(i, carry):
            f0 = pl.multiple_of(i * c, c)
            b0 = pl.multiple_of((n_chunks - 1 - i) * c, c)
            outf_ref[pl.ds(f0, c), :] = chunk(0, q_ref_[pl.ds(f0, c), :], k_ref_[pl.ds(f0, c), :], v_ref_[pl.ds(f0, c), :])
            outb_ref[pl.ds(b0, c), :] = chunk(1, q_ref_[pl.ds(b0, c), :], k_ref_[pl.ds(b0, c), :], v_ref_[pl.ds(b0, c), :])
            return carry
        lax.fori_loop(0, n_chunks, body, 0)

    run(lc // c, cq_ref, ck_ref, cv_ref, ocf_ref, ocb_ref)
    run(t // c, rq_ref, rk_ref, rv_ref, of_ref, ob_ref)

    havg = havg_ref[...]
    gn_g = gn_ref[...]

    def group_norm(o):
        o_hi = o.astype(BF16)
        o_lo = (o - o_hi.astype(F32)).astype(BF16)
        mu = (jnp.dot(o_hi, havg, preferred_element_type=F32) + jnp.dot(o_lo, havg, preferred_element_type=F32))
        d = o - mu
        var = jnp.dot((d * d).astype(BF16), havg, preferred_element_type=F32)
        return d * lax.rsqrt(var + EPS) * gn_g

    def finish(of, ob, gates):
        gf = gates[:, 0:RET_W].astype(F32)
        gb = gates[:, RET_W:2 * RET_W].astype(F32)
        return (group_norm(of) * _silu(gf) + group_norm(ob) * _silu(gb)).astype(BF16)

    o_ref[...] = finish(of_ref[...], ob_ref[...], gate_ref[...])
    oc_ref[...] = finish(ocf_ref[...], ocb_ref[...], cgate_ref[...])


def _ret_call(lg, lgl, gn_g, havg, rq, rk, rv, gate, cq, ck, cv, cgate):
    b, t, _ = rq.shape
    lc = cq.shape[1]
    row = lambda i: (0, 0)
    seq = lambda rows, w: pl.BlockSpec((None, rows, w), lambda i: (i, 0, 0))
    return pl.pallas_call(
        functools.partial(_ret_kernel, t=t, lc=lc),
        grid=(b,),
        in_specs=[pl.BlockSpec(memory_space=pltpu.SMEM),
                  pl.BlockSpec((2, RET_W), row), pl.BlockSpec((1, RET_W), row), pl.BlockSpec((RET_W, RET_W), row),
                  seq(t, RET_W), seq(t, RET_W), seq(t, RET_W), seq(t, 2 * RET_W),
                  seq(lc, RET_W), seq(lc, RET_W), seq(lc, RET_W), seq(lc, 2 * RET_W)],
        out_specs=[seq(t, RET_W), seq(lc, RET_W)],
        out_shape=[jax.ShapeDtypeStruct((b, t, RET_W), BF16), jax.ShapeDtypeStruct((b, lc, RET_W), BF16)],
        scratch_shapes=[pltpu.VMEM((2, RET_HEADS, RET_CHUNK, RET_CHUNK), F32),
                        pltpu.VMEM((2, RET_W, RET_W), F32),
                        pltpu.VMEM((t, RET_W), F32), pltpu.VMEM((t, RET_W), F32),
                        pltpu.VMEM((lc, RET_W), F32), pltpu.VMEM((lc, RET_W), F32)],
        compiler_params=_cparams(1),
        name="retention",
    )(lg, lgl, gn_g, havg, rq, rk, rv, gate, cq, ck, cv, cgate)


def _fourier_kernel(c_ref, s_ref, u_ref, o_ref):
    y = (jnp.dot(c_ref[...], u_ref[:, 0:FOUR_W], preferred_element_type=F32)
         + jnp.dot(s_ref[...], u_ref[:, FOUR_W:2 * FOUR_W], preferred_element_type=F32))
    o_ref[...] = y.astype(BF16)


def _fourier_call(cmat, smat_neg, u, *, tm):
    b, t, _ = u.shape
    mat = pl.BlockSpec((tm, t), lambda j, i: (j, 0))
    return pl.pallas_call(
        _fourier_kernel,
        grid=(t // tm, b),
        in_specs=[mat, mat, pl.BlockSpec((None, t, 2 * FOUR_W), lambda j, i: (i, 0, 0))],
        out_specs=pl.BlockSpec((None, tm, FOUR_W), lambda j, i: (i, j, 0)),
        out_shape=jax.ShapeDtypeStruct((b, t, FOUR_W), BF16),
        compiler_params=_cparams(2),
        name="fourier_mix",
    )(cmat, smat_neg, u)


def _mlp_kernel(x_ref, mod_ref, g_ref, att_ref, ret_ref, four_ref, wo_ref, w1_ref, w2_ref, o_ref, *, ff_chunk):
    m = (jnp.dot(att_ref[...], wo_ref[0:ATT_W, :], preferred_element_type=F32)
         + jnp.dot(ret_ref[...], wo_ref[ATT_W:ATT_W + RET_W, :], preferred_element_type=F32)
         + jnp.dot(four_ref[...], wo_ref[ATT_W + RET_W:, :], preferred_element_type=F32))
    x1 = x_ref[...] + mod_ref[2:3, :] * m
    ms = jnp.mean(x1 * x1, axis=-1, keepdims=True)
    hn = ((x1 * lax.rsqrt(ms + EPS)) * g_ref[...]) * (1.0 + mod_ref[4:5, :]) + mod_ref[3:4, :]
    hb = hn.astype(BF16)
    d_ff = w1_ref.shape[1]
    acc = jnp.zeros(x1.shape, F32)
    for c in range(d_ff // ff_chunk):
        hid = jnp.dot(hb, w1_ref[:, c * ff_chunk:(c + 1) * ff_chunk], preferred_element_type=F32)
        hid = jnp.square(jnp.maximum(hid, 0.0)).astype(BF16)
        acc = acc + jnp.dot(hid, w2_ref[c * ff_chunk:(c + 1) * ff_chunk, :], preferred_element_type=F32)
    o_ref[...] = x1 + mod_ref[5:6, :] * acc


def _mlp_call(x, mod, mod_row, g2, att, ret, four, w_out, w1, w2, *, tm):
    b, t, d = x.shape
    d_ff = w1.shape[1]
    row = lambda i, j: (0, 0)
    tok = lambda w: pl.BlockSpec((None, tm, w), lambda i, j: (i, j, 0))
    once = dict(pipeline_mode=pl.Buffered(1))
    return pl.pallas_call(
        functools.partial(_mlp_kernel, ff_chunk=1024),
        grid=(b, t // tm),
        in_specs=[tok(d),
                  pl.BlockSpec((None, N_MOD, d), lambda i, j: (mod_row(i), 0, 0)),
                  pl.BlockSpec((1, d), row),
                  tok(ATT_W), tok(RET_W), tok(FOUR_W),
                  pl.BlockSpec((d, d), row, **once),
                  pl.BlockSpec((d, d_ff), row, **once),
                  pl.BlockSpec((d_ff, d), row, **once)],
        out_specs=tok(d),
        out_shape=jax.ShapeDtypeStruct((b, t, d), F32),
        compiler_params=_cparams(2),
        name="out_projection_mlp",
    )(x, mod, g2, att, ret, four, w_out, w1, w2)


def _rope_tables(pos_groups, half):
    freqs = ROPE_BASE ** (-jnp.arange(half, dtype=F32) / half)
    cos, sin_next, sin_prev = [], [], []
    zeros = None
    for pos in pos_groups:
        ang = pos.astype(F32)[:, None] * freqs[None, :]
        c, s = jnp.cos(ang), jnp.sin(ang)
        zeros = jnp.zeros_like(s)
        cos += [c, c]
        sin_next += [-s, zeros]
        sin_prev += [zeros, s]
    reps = LANES // (2 * half * len(pos_groups))
    cat = lambda parts: jnp.tile(jnp.concatenate(parts, axis=1), (1, reps))
    return cat(cos), cat(sin_next), cat(sin_prev)


def _dft_tables(n):
    k = np.arange(n, dtype=np.int64)
    ang = 2.0 * np.pi * ((k[:, None] * k[None, :]) % n).astype(np.float64) / n
    return np.cos(ang), np.sin(ang)


def _block_diag(blocks):
    n = blocks.shape[-1]
    g = blocks.shape[-3]
    eye = jnp.eye(g, dtype=blocks.dtype)
    out = blocks[..., :, :, None, :] * eye[:, None, :, None]
    return out.reshape(blocks.shape[:-3] + (g * n, g * n))


def kernel(x, c, ctx, c_ctx, w_mod, b_mod, norm1_g, norm2_g, w_in, w_out, q_norm_g, k_norm_g, attn_sink,
           ret_decay_logit, ret_gn_g, fourier_w, w_ff1, w_ff2):
    b, s, d = x.shape
    lc = ctx.shape[1]
    depth = w_mod.shape[0]
    fdim = fourier_w.shape[-1]

    n_rows = -(-(b + 1) // 16) * 16
    cvec = jnp.zeros((n_rows, d), F32).at[:b].set(c).at[b].set(c_ctx)
    mod = _mod_call(cvec, w_mod, b_mod).reshape(depth, n_rows, N_MOD, d)

    pos = jnp.arange(s)
    tabs_lat = _rope_tables([pos // GRID_W, pos % GRID_W], HEAD_DIM // 4) + _rope_tables([pos], HEAD_DIM // 2)
    tabs_ctx = tuple(jnp.zeros((lc, LANES), F32) for _ in range(6))
    head_id = np.arange(256) // HEAD_DIM
    hsum = jnp.asarray(head_id[:, None] == head_id[None, :], BF16)
    havg = jnp.asarray((head_id[:, None] == head_id[None, :]) / HEAD_DIM, BF16)
    c_ch, s_ch = _dft_tables(fdim)
    eye_g = np.eye(FOURIER_GROUPS)
    c_bd = jnp.asarray(np.kron(eye_g, c_ch), F32)
    s_bd = jnp.asarray(np.kron(eye_g, s_ch), F32)

    def pos_tables(n):
        cn, sn = _dft_tables(n)
        norm = 1.0 / np.sqrt(float(n) * fdim)
        return jnp.asarray(cn * norm, F32).astype(BF16), jnp.asarray(-sn * norm, F32).astype(BF16)

    cl_s, sl_s = pos_tables(s)
    cl_c, sl_c = pos_tables(lc)

    wc, ws = _fourier_prep_call(_block_diag(fourier_w.astype(F32)), c_bd, s_bd)
    wcs = jnp.concatenate([wc, ws], axis=-1).astype(BF16)

    w_in_b = w_in.astype(BF16)
    w_out_b = w_out.astype(BF16)
    w1_b = w_ff1.astype(BF16)
    w2_b = w_ff2.astype(BF16)
    gq = jnp.tile(q_norm_g, (1, 256 // HEAD_DIM)).reshape(depth, 1, 256)
    gk = jnp.tile(k_norm_g, (1, LANES // HEAD_DIM)).reshape(depth, 1, LANES)
    log_gamma = -jax.nn.softplus(-ret_decay_logit.astype(F32))
    log_gamma_lanes = jnp.repeat(log_gamma, HEAD_DIM, axis=-1)

    lat_row = lambda i: i
    ctx_row = lambda i: b

    h = ctx
    for l in range(depth):
        need_ctx = l < depth - 1
        common = (norm1_g[l].reshape(1, d), w_in_b[l], gq[l], gk[l], hsum, wcs[l])
        q, kz, vz, rq, rk, rv, gate, u = _inproj_call(x, mod[l], lat_row, *common, tabs_lat, rope=True, tm=512)
        cq, ckz, cvz, crq, crk, crv, cgate, cu = _inproj_call(h, mod[l], ctx_row, *common, tabs_ctx, rope=False, tm=lc)

        att = _attn_call(attn_sink[l], q, kz, vz, ckz, cvz, band=True)
        ret, ret_c = _ret_call(log_gamma[l], log_gamma_lanes[l], ret_gn_g[l].reshape(1, RET_W), havg,
                               rq, rk, rv, gate, crq, crk, crv, cgate)
        four = _fourier_call(cl_s, sl_s, u, tm=512)
        x = _mlp_call(x, mod[l], lat_row, norm2_g[l].reshape(1, d), att, ret, four, w_out_b[l], w1_b[l], w2_b[l], tm=512)
        if need_ctx:
            att_c = _attn_call(attn_sink[l], cq, ckz, cvz, ckz, cvz, band=False)
            four_c = _fourier_call(cl_c, sl_c, cu, tm=lc)
            h = _mlp_call(h, mod[l], ctx_row, norm2_g[l].reshape(1, d), att_c, ret_c, four_c,
                          w_out_b[l], w1_b[l], w2_b[l], tm=lc)
    return x
```

```python
import functools

import numpy as np
import jax
import jax.numpy as jnp
from jax import lax
from jax.experimental import pallas as pl
from jax.experimental.pallas import tpu as pltpu

F32 = jnp.float32
BF16 = jnp.bfloat16

HEAD_DIM = 64
LANES = 128
GRID_W = 64
N_MOD = 6
ATT_Q_HEADS = 8
ATT_KV_HEADS = 2
WINDOW = 128
BLOCK = 128
RET_HEADS = 4
RET_CHUNK = 128
FOURIER_GROUPS = 4
ROPE_BASE = 10000.0
EPS = 1e-6
NEG_INF = -1e30
VMEM_LIMIT_BYTES = 56 * 1024 * 1024

ATT_W = ATT_Q_HEADS * HEAD_DIM
KV_W = ATT_KV_HEADS * HEAD_DIM
RET_W = RET_HEADS * HEAD_DIM
FOUR_W = 256
O_Q, O_K, O_V = 0, ATT_W, ATT_W + KV_W
O_RQ = ATT_W + 2 * KV_W
O_RK, O_RV, O_GF, O_GB = O_RQ + RET_W, O_RQ + 2 * RET_W, O_RQ + 3 * RET_W, O_RQ + 4 * RET_W
O_FU = O_RQ + 5 * RET_W
PROJ_W = O_FU + FOUR_W
LOG2E = 1.4426950408889634
Q_SCALE = HEAD_DIM ** -0.5 * LOG2E


def _cparams(n_axes, flags=None):
    return pltpu.CompilerParams(dimension_semantics=("arbitrary",) * n_axes,
                                vmem_limit_bytes=VMEM_LIMIT_BYTES, flags=flags)


def _silu(v):
    return v / (1.0 + jnp.exp(-v))


def _mod_kernel(c_ref, w_ref, b_ref, o_ref):
    s = _silu(c_ref[...]).astype(BF16)
    o_ref[...] = jnp.dot(s, w_ref[...].astype(BF16), preferred_element_type=F32) + b_ref[...]


def _mod_call(cvec, w_mod, b_mod):
    depth, d, n = w_mod.shape
    r = cvec.shape[0]
    tn = 1536
    return pl.pallas_call(
        _mod_kernel,
        grid=(depth, n // tn),
        in_specs=[pl.BlockSpec((r, d), lambda l, j: (0, 0)),
                  pl.BlockSpec((None, d, tn), lambda l, j: (l, 0, j)),
                  pl.BlockSpec((None, 1, tn), lambda l, j: (l, 0, j))],
        out_specs=pl.BlockSpec((None, r, tn), lambda l, j: (l, 0, j)),
        out_shape=jax.ShapeDtypeStruct((depth, r, n), F32),
        compiler_params=_cparams(2),
        name="mod_vectors",
    )(cvec, w_mod, b_mod.reshape(depth, 1, n))


def _fourier_prep_kernel(w_ref, c_ref, s_ref, wc_ref, ws_ref):
    w = w_ref[...]
    wc_ref[...] = jnp.dot(c_ref[...], w, preferred_element_type=F32, precision=lax.Precision.HIGHEST)
    ws_ref[...] = jnp.dot(s_ref[...], w, preferred_element_type=F32, precision=lax.Precision.HIGHEST)


def _fourier_prep_call(w_bd, c_bd, s_bd):
    depth, n, _ = w_bd.shape
    spec_w = pl.BlockSpec((None, n, n), lambda l: (l, 0, 0))
    spec_t = pl.BlockSpec((n, n), lambda l: (0, 0))
    return pl.pallas_call(
        _fourier_prep_kernel,
        grid=(depth,),
        in_specs=[spec_w, spec_t, spec_t],
        out_specs=[spec_w, spec_w],
        out_shape=[jax.ShapeDtypeStruct((depth, n, n), F32)] * 2,
        compiler_params=_cparams(1),
        name="fourier_prep",
    )(w_bd, c_bd, s_bd)


def _rope(v, cos, sin_next, sin_prev, shift):
    return (v * cos + pltpu.roll(v, LANES - shift, 1) * sin_next + pltpu.roll(v, shift, 1) * sin_prev)


def _even_odd_variants(v, ones_lane=False):
    lane = lax.broadcasted_iota(jnp.int32, v.shape, 1)
    lo = lane < HEAD_DIM
    vr = pltpu.roll(v, HEAD_DIM, 1)
    pad_lo = jnp.where(lane == 0, 1.0, 0.0) if ones_lane else jnp.zeros_like(v)
    pad_hi = jnp.where(lane == HEAD_DIM, 1.0, 0.0) if ones_lane else jnp.zeros_like(v)
    return (jnp.where(lo, v, pad_hi), jnp.where(lo, pad_lo, vr), jnp.where(lo, vr, pad_hi), jnp.where(lo, pad_lo, v))


def _inproj_kernel(x_ref, mod_ref, g_ref, w_ref, gq_ref, gk_ref, hsum_ref, wcs_ref,
                   ca_ref, san_ref, sap_ref, ct_ref, stn_ref, stp_ref,
                   q_ref, kz_ref, vz_ref, rq_ref, rk_ref, rv_ref, gate_ref, u_ref, *, rope, sub):
    shift = mod_ref[0:1, :]
    scale1 = 1.0 + mod_ref[1:2, :]
    hsum = hsum_ref[...]
    inv_d = 1.0 / HEAD_DIM

    def head_norm(v, gain):
        ssq = jnp.dot((v * v).astype(BF16), hsum, preferred_element_type=F32)
        return v * lax.rsqrt(ssq * inv_d + EPS) * gain

    for r0 in range(0, x_ref.shape[0], sub):
        rows = slice(r0, r0 + sub)
        x = x_ref[rows, :]
        ms = jnp.mean(x * x, axis=-1, keepdims=True)
        a = ((x * lax.rsqrt(ms + EPS)) * g_ref[...]) * scale1 + shift
        pall = jnp.dot(a.astype(BF16), w_ref[...], preferred_element_type=F32)
        if rope:
            att_tab = (ca_ref[rows, :], san_ref[rows, :], sap_ref[rows, :], HEAD_DIM // 4)
            ret_tab = (ct_ref[rows, :], stn_ref[rows, :], stp_ref[rows, :], HEAD_DIM // 2)

        for j in range(ATT_W // 256):
            qn = head_norm(pall[:, O_Q + j * 256:O_Q + (j + 1) * 256], gq_ref[...])
            for c in range(2):
                col = qn[:, c * LANES:(c + 1) * LANES]
                if rope:
                    col = _rope(col, *att_tab)
                lo = j * 256 + c * LANES
                q_ref[rows, lo:lo + LANES] = (col * Q_SCALE).astype(BF16)

        k = pall[:, O_K:O_V]
        ssq = jnp.dot((k * k).astype(BF16), hsum[0:LANES, 0:LANES], preferred_element_type=F32)
        k = k * lax.rsqrt(ssq * inv_d + EPS) * gk_ref[...]
        if rope:
            k = _rope(k, *att_tab)
        for i, var in enumerate(_even_odd_variants(k)):
            kz_ref[rows, i * LANES:(i + 1) * LANES] = var.astype(BF16)
        for i, var in enumerate(_even_odd_variants(pall[:, O_V:O_RQ], ones_lane=True)):
            vz_ref[rows, i * LANES:(i + 1) * LANES] = var.astype(BF16)

        for c in range(RET_W // LANES):
            qc = pall[:, O_RQ + c * LANES:O_RQ + (c + 1) * LANES]
            kc = pall[:, O_RK + c * LANES:O_RK + (c + 1) * LANES] * HEAD_DIM ** -0.5
            if rope:
                qc = _rope(qc, *ret_tab)
                kc = _rope(kc, *ret_tab)
            rq_ref[rows, c * LANES:(c + 1) * LANES] = qc.astype(BF16)
            rk_ref[rows, c * LANES:(c + 1) * LANES] = kc.astype(BF16)
        rv_ref[rows, :] = pall[:, O_RV:O_GF].astype(BF16)
        gate_ref[rows, :] = pall[:, O_GF:O_FU].astype(BF16)

        fu = pall[:, O_FU:PROJ_W].astype(BF16)
        u_ref[rows, :] = jnp.dot(fu, wcs_ref[...], preferred_element_type=F32).astype(BF16)


def _inproj_call(layer, x, mod, mod_row, g1, w_in, gq, gk, hsum, wcs, tabs, *, rope, tm):
    b, t, d = x.shape
    per_layer = lambda *shape: pl.BlockSpec((None,) + shape, lambda i, j: (layer,) + (0,) * len(shape))
    tab = pl.BlockSpec((tm, LANES), lambda i, j: (j, 0))
    tok = lambda w: pl.BlockSpec((None, tm, w), lambda i, j: (i, j, 0))
    out_w = (ATT_W, 4 * LANES, 4 * LANES, RET_W, RET_W, RET_W, 2 * RET_W, 2 * FOUR_W)
    return pl.pallas_call(
        functools.partial(_inproj_kernel, rope=rope, sub=tm),
        grid=(b, t // tm),
        in_specs=[tok(d),
                  pl.BlockSpec((None, None, N_MOD, d), lambda i, j: (layer, mod_row(i), 0, 0)),
                  per_layer(1, d),
                  per_layer(d, PROJ_W),
                  per_layer(1, 256),
                  per_layer(1, LANES),
                  pl.BlockSpec((256, 256), lambda i, j: (0, 0)),
                  per_layer(FOUR_W, 2 * FOUR_W),
                  tab, tab, tab, tab, tab, tab],
        out_specs=[tok(w) for w in out_w],
        out_shape=[jax.ShapeDtypeStruct((b, t, w), BF16) for w in out_w],
        compiler_params=_cparams(2),
        name="in_projection",
    )(x, mod, g1, w_in, gq, gk, hsum, wcs, *tabs)


def _attn_kernel(sink_ref, bias_ref, q_ref, kz_ref, vz_ref, kcz_ref, vcz_ref, o_ref, *, band, t, tq):
    n = pl.program_id(1)
    contract_last = (((1,), (1,)), ((), ()))
    if band:
        span = tq + 2 * WINDOW
        start = pl.multiple_of(jnp.clip(n * tq - WINDOW, 0, t - span), BLOCK)
        last = t // tq - 1
        bias = bias_ref[jnp.where(n == 0, 0, jnp.where(n == last, 2, 1))]
    upper = lax.broadcasted_iota(jnp.int32, (2 * tq, 1), 0) < tq
    lane_lo = lax.broadcasted_iota(jnp.int32, (2 * tq, LANES), 1) < HEAD_DIM
    raw = {}
    for h in range(ATT_KV_HEADS):
        qq = jnp.concatenate([q_ref[:, (2 * h) * LANES:(2 * h + 1) * LANES],
                              q_ref[:, (2 * h + 1) * LANES:(2 * h + 2) * LANES]], axis=0)
        for par in range(2):
            col = (2 * h + par) * LANES
            s_ctx = lax.dot_general(qq, kcz_ref[:, col:col + LANES], contract_last, preferred_element_type=F32)
            s_band = None
            if band:
                s_band = lax.dot_general(qq, kz_ref[pl.ds(start, span), col:col + LANES], contract_last,
                                         preferred_element_type=F32)
            raw[h, par] = (s_ctx, s_band)
    for h in range(ATT_KV_HEADS):
        scaled = []
        for par in range(2):
            col = (2 * h + par) * LANES
            s_ctx, s_band = raw[h, par]
            sink = jnp.where(upper, sink_ref[4 * h + par], sink_ref[4 * h + 2 + par]) * LOG2E
            m = jnp.maximum(jnp.max(s_ctx, axis=-1, keepdims=True), sink)
            if band:
                s_band = s_band + bias
                m = jnp.maximum(m, jnp.max(s_band, axis=-1, keepdims=True))
            acc = jnp.dot(jnp.exp2(s_ctx - m).astype(BF16), vcz_ref[:, col:col + LANES], preferred_element_type=F32)
            if band:
                acc = acc + jnp.dot(jnp.exp2(s_band - m).astype(BF16), vz_ref[pl.ds(start, span), col:col + LANES],
                                    preferred_element_type=F32)
            ones = HEAD_DIM if par == 0 else 0
            den = acc[:, ones:ones + 1] + jnp.exp2(sink - m)
            scaled.append(acc * (1.0 / den))
        out = jnp.where(lane_lo, scaled[0], scaled[1])
        o_ref[:, (2 * h) * LANES:(2 * h + 1) * LANES] = out[0:tq].astype(BF16)
        o_ref[:, (2 * h + 1) * LANES:(2 * h + 2) * LANES] = out[tq:2 * tq].astype(BF16)


def _attn_call(sink, bias, q, kz, vz, kcz, vcz, *, band):
    b, t, _ = q.shape
    lc = kcz.shape[1]
    tq = bias.shape[1] // 2
    full = lambda rows: pl.BlockSpec((None, rows, 4 * LANES), lambda i, j: (i, 0, 0))
    blk = pl.BlockSpec((None, tq, ATT_W), lambda i, j: (i, j, 0))
    return pl.pallas_call(
        functools.partial(_attn_kernel, band=band, t=t, tq=tq),
        grid=(b, t // tq),
        in_specs=[pl.BlockSpec(memory_space=pltpu.SMEM),
                  pl.BlockSpec(bias.shape, lambda i, j: (0, 0, 0)),
                  blk, full(kz.shape[1]), full(vz.shape[1]), full(lc), full(lc)],
        out_specs=blk,
        out_shape=jax.ShapeDtypeStruct((b, t, ATT_W), BF16),
        compiler_params=_cparams(2),
        name="window_attention" if band else "context_attention",
    )(sink, bias, q, kz, vz, kcz, vcz)


def _window_bias(tq):
    span = tq + 2 * WINDOW
    r = np.arange(2 * tq)[:, None] % tq
    c = np.arange(span)[None, :]
    cases = [np.where(np.abs(c - r - off) <= WINDOW, 0.0, NEG_INF) for off in (0, WINDOW, 2 * WINDOW)]
    return jnp.asarray(np.stack(cases), F32)


def _ret_kernel(lg_ref, lgl_ref, gn_ref, havg_ref,
                rq_ref, rk_ref, rv_ref, gate_ref, cq_ref, ck_ref, cv_ref, cgate_ref,
                o_ref, oc_ref,
                dec_ref, st_ref, of_ref, ob_ref, ocf_ref, ocb_ref, *, t, lc):
    c = RET_CHUNK

    @pl.when(pl.program_id(0) == 0)
    def _():
        ri = lax.broadcasted_iota(jnp.int32, (c, c), 0)
        ci = lax.broadcasted_iota(jnp.int32, (c, c), 1)
        diff = (ri - ci).astype(F32)
        for h in range(RET_HEADS):
            dec_ref[0, :, h * c:(h + 1) * c] = jnp.where(ri >= ci, jnp.exp(jnp.maximum(diff, 0.0) * lg_ref[0, h]), 0.0)
            dec_ref[1, :, h * c:(h + 1) * c] = jnp.where(ci >= ri, jnp.exp(jnp.maximum(-diff, 0.0) * lg_ref[1, h]), 0.0)

    idx = lax.broadcasted_iota(jnp.int32, (c, RET_W), 0).astype(F32)
    lgf = lgl_ref[0:1, :]
    lgb = lgl_ref[1:2, :]
    xi = (jnp.exp((idx + 1.0) * lgf), jnp.exp((c - idx) * lgb))
    zeta = (jnp.exp((c - 1.0 - idx) * lgf), jnp.exp(idx * lgb))
    cdec = (jnp.exp(c * lgf), jnp.exp(c * lgb))
    lane_head = lax.broadcasted_iota(jnp.int32, (1, RET_W), 1) // HEAD_DIM
    head_mask = [lane_head == h for h in range(RET_HEADS)]
    blockdiag = (lax.broadcasted_iota(jnp.int32, (RET_W, RET_W), 0) // HEAD_DIM
                 == lax.broadcasted_iota(jnp.int32, (RET_W, RET_W), 1) // HEAD_DIM)
    contract_last = (((1,), (1,)), ((), ()))
    contract_first = (((0,), (0,)), ((), ()))

    def by_head(v):
        zero = jnp.zeros_like(v)
        return jnp.concatenate([jnp.where(head_mask[h], v, zero) for h in range(RET_HEADS)], axis=0)

    def chunk(direction, q, k, v):
        s = lax.dot_general(q, by_head(k), contract_last, preferred_element_type=F32)
        p = (s * dec_ref[direction]).astype(BF16)
        inner = jnp.dot(p, by_head(v), preferred_element_type=F32)
        st = st_ref[direction]
        cross = jnp.dot(q, st.astype(BF16), preferred_element_type=F32) * xi[direction]
        kd = (k.astype(F32) * zeta[direction]).astype(BF16)
        kv = lax.dot_general(kd, v, contract_first, preferred_element_type=F32)
        st_ref[direction] = st * cdec[direction] + jnp.where(blockdiag, kv, 0.0)
        return inner + cross

    st_ref[...] = jnp.zeros_like(st_ref)

    def run(n_chunks, q_ref_, k_ref_, v_ref_, outf_ref, outb_ref):
        def body(i, carry):
            f0 = pl.multiple_of(i * c, c)
            b0 = pl.multiple_of((n_chunks - 1 - i) * c, c)
            outf_ref[pl.ds(f0, c), :] = chunk(0, q_ref_[pl.ds(f0, c), :], k_ref_[pl.ds(f0, c), :], v_ref_[pl.ds(f0, c), :])
            outb_ref[pl.ds(b0, c), :] = chunk(1, q_ref_[pl.ds(b0, c), :], k_ref_[pl.ds(b0, c), :], v_ref_[pl.ds(b0, c), :])
            return carry
        lax.fori_loop(0, n_chunks, body, 0, unroll=min(4, n_chunks))

    run(lc // c, cq_ref, ck_ref, cv_ref, ocf_ref, ocb_ref)
    run(t // c, rq_ref, rk_ref, rv_ref, of_ref, ob_ref)

    havg = havg_ref[...]
    gn_g = gn_ref[...]

    def group_norm(o):
        mu = jnp.dot(o.astype(BF16), havg, preferred_element_type=F32)
        d = o - mu
        var = jnp.dot((d * d).astype(BF16), havg, preferred_element_type=F32)
        return d * lax.rsqrt(var + EPS) * gn_g

    def finish(of, ob, gates):
        gf = gates[:, 0:RET_W].astype(F32)
        gb = gates[:, RET_W:2 * RET_W].astype(F32)
        return (group_norm(of) * _silu(gf) + group_norm(ob) * _silu(gb)).astype(BF16)

    o_ref[...] = finish(of_ref[...], ob_ref[...], gate_ref[...])
    oc_ref[...] = finish(ocf_ref[...], ocb_ref[...], cgate_ref[...])


def _ret_call(lg, lgl, gn_g, havg, rq, rk, rv, gate, cq, ck, cv, cgate):
    b, t, _ = rq.shape
    lc = cq.shape[1]
    row = lambda i: (0, 0)
    seq = lambda rows, w: pl.BlockSpec((None, rows, w), lambda i: (i, 0, 0))
    return pl.pallas_call(
        functools.partial(_ret_kernel, t=t, lc=lc),
        grid=(b,),
        in_specs=[pl.BlockSpec(memory_space=pltpu.SMEM),
                  pl.BlockSpec((2, RET_W), row), pl.BlockSpec((1, RET_W), row), pl.BlockSpec((RET_W, RET_W), row),
                  seq(t, RET_W), seq(t, RET_W), seq(t, RET_W), seq(t, 2 * RET_W),
                  seq(lc, RET_W), seq(lc, RET_W), seq(lc, RET_W), seq(lc, 2 * RET_W)],
        out_specs=[seq(t, RET_W), seq(lc, RET_W)],
        out_shape=[jax.ShapeDtypeStruct((b, t, RET_W), BF16), jax.ShapeDtypeStruct((b, lc, RET_W), BF16)],
        scratch_shapes=[pltpu.VMEM((2, RET_CHUNK, RET_HEADS * RET_CHUNK), F32),
                        pltpu.VMEM((2, RET_W, RET_W), F32),
                        pltpu.VMEM((t, RET_W), F32), pltpu.VMEM((t, RET_W), F32),
                        pltpu.VMEM((lc, RET_W), F32), pltpu.VMEM((lc, RET_W), F32)],
        compiler_params=_cparams(1),
        name="retention",
    )(lg, lgl, gn_g, havg, rq, rk, rv, gate, cq, ck, cv, cgate)


def _fourier_kernel(c_ref, s_ref, flip_ref, u_ref, o_ref, *, t, blk):
    half = t // 2
    a = jnp.dot(c_ref[...], u_ref[:, 0:FOUR_W], preferred_element_type=F32)
    bs = jnp.dot(s_ref[...], u_ref[:, FOUR_W:2 * FOUR_W], preferred_element_type=F32)
    o_ref[0:half, :] = (a[0:half] + bs[0:half]).astype(BF16)
    mid = (a[half:half + 1] + bs[half:half + 1]).astype(BF16)
    mirror = (a[0:half] - bs[0:half]).astype(BF16)
    row0 = lax.broadcasted_iota(jnp.int32, (blk, FOUR_W), 0) == 0
    n_blk = half // blk
    for i in range(n_blk):
        src = mirror[(n_blk - 1 - i) * blk:(n_blk - i) * blk]
        rev = jnp.dot(flip_ref[...], src, preferred_element_type=F32).astype(BF16)
        first = mid if i == 0 else mirror[(n_blk - i) * blk:(n_blk - i) * blk + 1]
        o_ref[half + i * blk:half + (i + 1) * blk, :] = jnp.where(row0, first, rev)


def _fourier_call(cmat, smat_neg, flip, u):
    b, t, _ = u.shape
    rows = cmat.shape[0]
    blk = flip.shape[0]
    const = lambda *shape: pl.BlockSpec(shape, lambda i: (0,) * len(shape))
    return pl.pallas_call(
        functools.partial(_fourier_kernel, t=t, blk=blk),
        grid=(b,),
        in_specs=[const(rows, t), const(rows, t), const(blk, blk),
                  pl.BlockSpec((None, t, 2 * FOUR_W), lambda i: (i, 0, 0))],
        out_specs=pl.BlockSpec((None, t, FOUR_W), lambda i: (i, 0, 0)),
        out_shape=jax.ShapeDtypeStruct((b, t, FOUR_W), BF16),
        compiler_params=_cparams(1),
        name="fourier_mix",
    )(cmat, smat_neg, flip, u)


def _mlp_kernel(x_ref, mod_ref, g_ref, att_ref, ret_ref, four_ref, wo_ref, w1_ref, w2_ref, o_ref, *, ff_chunk):
    m = (jnp.dot(att_ref[...], wo_ref[0:ATT_W, :], preferred_element_type=F32)
         + jnp.dot(ret_ref[...], wo_ref[ATT_W:ATT_W + RET_W, :], preferred_element_type=F32)
         + jnp.dot(four_ref[...], wo_ref[ATT_W + RET_W:, :], preferred_element_type=F32))
    x1 = x_ref[...] + mod_ref[2:3, :] * m
    ms = jnp.mean(x1 * x1, axis=-1, keepdims=True)
    hn = ((x1 * lax.rsqrt(ms + EPS)) * g_ref[...]) * (1.0 + mod_ref[4:5, :]) + mod_ref[3:4, :]
    hb = hn.astype(BF16)
    d_ff = w1_ref.shape[1]
    acc = jnp.zeros(x1.shape, F32)
    for c in range(d_ff // ff_chunk):
        hid = jnp.dot(hb, w1_ref[:, c * ff_chunk:(c + 1) * ff_chunk], preferred_element_type=F32)
        hid = jnp.square(jnp.maximum(hid, 0.0)).astype(BF16)
        acc = acc + jnp.dot(hid, w2_ref[c * ff_chunk:(c + 1) * ff_chunk, :], preferred_element_type=F32)
    o_ref[...] = x1 + mod_ref[5:6, :] * acc


def _mlp_call(layer, x, mod, mod_row, g2, att, ret, four, w_out, w1, w2, *, tm):
    b, t, d = x.shape
    d_ff = w1.shape[-1]
    tok = lambda w: pl.BlockSpec((None, tm, w), lambda i, j: (i, j, 0))
    once = dict(pipeline_mode=pl.Buffered(1))
    per_layer = lambda *shape, **kw: pl.BlockSpec((None,) + shape, lambda i, j: (layer,) + (0,) * len(shape), **kw)
    return pl.pallas_call(
        functools.partial(_mlp_kernel, ff_chunk=1024),
        grid=(b, t // tm),
        in_specs=[tok(d),
                  pl.BlockSpec((None, None, N_MOD, d), lambda i, j: (layer, mod_row(i), 0, 0)),
                  per_layer(1, d),
                  tok(ATT_W), tok(RET_W), tok(FOUR_W),
                  per_layer(d, d, **once),
                  per_layer(d, d_ff, **once),
                  per_layer(d_ff, d, **once)],
        out_specs=tok(d),
        out_shape=jax.ShapeDtypeStruct((b, t, d), F32),
        compiler_params=_cparams(2),
        name="out_projection_mlp",
    )(x, mod, g2, att, ret, four, w_out, w1, w2)


def _rope_tables(pos_groups, half):
    freqs = ROPE_BASE ** (-jnp.arange(half, dtype=F32) / half)
    cos, sin_next, sin_prev = [], [], []
    zeros = None
    for pos in pos_groups:
        ang = pos.astype(F32)[:, None] * freqs[None, :]
        c, s = jnp.cos(ang), jnp.sin(ang)
        zeros = jnp.zeros_like(s)
        cos += [c, c]
        sin_next += [-s, zeros]
        sin_prev += [zeros, s]
    reps = LANES // (2 * half * len(pos_groups))
    cat = lambda parts: jnp.tile(jnp.concatenate(parts, axis=1), (1, reps))
    return cat(cos), cat(sin_next), cat(sin_prev)


def _dft_tables(n):
    k = np.arange(n, dtype=np.int64)
    ang = 2.0 * np.pi * ((k[:, None] * k[None, :]) % n).astype(np.float64) / n
    return np.cos(ang), np.sin(ang)


def _block_diag(blocks):
    n = blocks.shape[-1]
    g = blocks.shape[-3]
    eye = jnp.eye(g, dtype=blocks.dtype)
    out = blocks[..., :, :, None, :] * eye[:, None, :, None]
    return out.reshape(blocks.shape[:-3] + (g * n, g * n))


def kernel(x, c, ctx, c_ctx, w_mod, b_mod, norm1_g, norm2_g, w_in, w_out, q_norm_g, k_norm_g, attn_sink,
           ret_decay_logit, ret_gn_g, fourier_w, w_ff1, w_ff2):
    b, s, d = x.shape
    lc = ctx.shape[1]
    depth = w_mod.shape[0]
    fdim = fourier_w.shape[-1]

    n_rows = -(-(b + 1) // 16) * 16
    cvec = jnp.zeros((n_rows, d), F32).at[:b].set(c).at[b].set(c_ctx)
    mod = _mod_call(cvec, w_mod, b_mod).reshape(depth, n_rows, N_MOD, d)

    pos = jnp.arange(s)
    tabs_lat = _rope_tables([pos // GRID_W, pos % GRID_W], HEAD_DIM // 4) + _rope_tables([pos], HEAD_DIM // 2)
    tabs_ctx = tuple(jnp.zeros((lc, LANES), F32) for _ in range(6))
    head_id = np.arange(256) // HEAD_DIM
    hsum = jnp.asarray(head_id[:, None] == head_id[None, :], BF16)
    havg = jnp.asarray((head_id[:, None] == head_id[None, :]) / HEAD_DIM, BF16)
    c_ch, s_ch = _dft_tables(fdim)
    eye_g = np.eye(FOURIER_GROUPS)
    c_bd = jnp.asarray(np.kron(eye_g, c_ch), F32)
    s_bd = jnp.asarray(np.kron(eye_g, s_ch), F32)

    def pos_tables(n):
        cn, sn = _dft_tables(n)
        norm = 1.0 / np.sqrt(float(n) * fdim)
        rows = n // 2 + 16
        blk = min(256, n // 2)
        r = np.arange(blk)
        flip = ((r[:, None] + r[None, :]) == blk).astype(np.float32)
        return (jnp.asarray(cn[:rows] * norm, F32).astype(BF16), jnp.asarray(-sn[:rows] * norm, F32).astype(BF16),
                jnp.asarray(flip, BF16))

    ftab_s = pos_tables(s)
    ftab_c = pos_tables(lc)
    win_bias = _window_bias(2 * BLOCK)

    wc, ws = _fourier_prep_call(_block_diag(fourier_w.astype(F32)), c_bd, s_bd)
    wcs = jnp.concatenate([wc, ws], axis=-1).astype(BF16)

    w_in_b = w_in.astype(BF16)
    w_out_b = w_out.astype(BF16)
    w1_b = w_ff1.astype(BF16)
    w2_b = w_ff2.astype(BF16)
    g1 = norm1_g.reshape(depth, 1, d)
    g2 = norm2_g.reshape(depth, 1, d)
    gq = jnp.tile(q_norm_g, (1, 256 // HEAD_DIM)).reshape(depth, 1, 256)
    gk = jnp.tile(k_norm_g, (1, LANES // HEAD_DIM)).reshape(depth, 1, LANES)
    log_gamma = -jax.nn.softplus(-ret_decay_logit.astype(F32))
    log_gamma_lanes = jnp.repeat(log_gamma, HEAD_DIM, axis=-1)

    lat_row = lambda i: i
    ctx_row = lambda i: b

    h = ctx
    for l in range(depth):
        need_ctx = l < depth - 1
        common = (g1, w_in_b, gq, gk, hsum, wcs)
        q, kz, vz, rq, rk, rv, gate, u = _inproj_call(l, x, mod, lat_row, *common, tabs_lat, rope=True, tm=512)
        cq, ckz, cvz, crq, crk, crv, cgate, cu = _inproj_call(l, h, mod, ctx_row, *common, tabs_ctx, rope=False, tm=lc)

        att = _attn_call(attn_sink[l], win_bias, q, kz, vz, ckz, cvz, band=True)
        ret, ret_c = _ret_call(log_gamma[l], log_gamma_lanes[l], ret_gn_g[l].reshape(1, RET_W), havg,
                               rq, rk, rv, gate, crq, crk, crv, cgate)
        four = _fourier_call(*ftab_s, u)
        x = _mlp_call(l, x, mod, lat_row, g2, att, ret, four, w_out_b, w1_b, w2_b, tm=512)
        if need_ctx:
            att_c = _attn_call(attn_sink[l], win_bias, cq, ckz, cvz, ckz, cvz, band=False)
            four_c = _fourier_call(*ftab_c, cu)
            h = _mlp_call(l, h, mod, ctx_row, g2, att_c, ret_c, four_c, w_out_b, w1_b, w2_b, tm=lc)
    return x
```

```python
import functools

import numpy as np
import jax
import jax.numpy as jnp
from jax import lax
from jax.experimental import pallas as pl
from jax.experimental.pallas import tpu as pltpu

F32 = jnp.float32
BF16 = jnp.bfloat16

HEAD_DIM = 64
LANES = 128
GRID_W = 64
N_MOD = 6
ATT_Q_HEADS = 8
ATT_KV_HEADS = 2
WINDOW = 128
BLOCK = 128
RET_HEADS = 4
RET_CHUNK = 128
FOURIER_GROUPS = 4
ROPE_BASE = 10000.0
EPS = 1e-6
NEG_INF = -1e30
VMEM_LIMIT_BYTES = 56 * 1024 * 1024

ATT_W = ATT_Q_HEADS * HEAD_DIM
KV_W = ATT_KV_HEADS * HEAD_DIM
RET_W = RET_HEADS * HEAD_DIM
FOUR_W = 256
O_Q, O_K, O_V = 0, ATT_W, ATT_W + KV_W
O_RQ = ATT_W + 2 * KV_W
O_RK, O_RV, O_GF, O_GB = O_RQ + RET_W, O_RQ + 2 * RET_W, O_RQ + 3 * RET_W, O_RQ + 4 * RET_W
O_FU = O_RQ + 5 * RET_W
PROJ_W = O_FU + FOUR_W
LOG2E = 1.4426950408889634
Q_SCALE = HEAD_DIM ** -0.5 * LOG2E


def _cparams(n_axes, flags=None):
    return pltpu.CompilerParams(dimension_semantics=("arbitrary",) * n_axes,
                                vmem_limit_bytes=VMEM_LIMIT_BYTES, flags=flags)


def _silu(v):
    return v / (1.0 + jnp.exp(-v))


def _mod_kernel(c_ref, w_ref, b_ref, o_ref):
    s = _silu(c_ref[...]).astype(BF16)
    o_ref[...] = jnp.dot(s, w_ref[...].astype(BF16), preferred_element_type=F32) + b_ref[...]


def _mod_call(cvec, w_mod, b_mod):
    depth, d, n = w_mod.shape
    r = cvec.shape[0]
    tn = 1536
    return pl.pallas_call(
        _mod_kernel,
        grid=(depth, n // tn),
        in_specs=[pl.BlockSpec((r, d), lambda l, j: (0, 0)),
                  pl.BlockSpec((None, d, tn), lambda l, j: (l, 0, j)),
                  pl.BlockSpec((None, 1, tn), lambda l, j: (l, 0, j))],
        out_specs=pl.BlockSpec((None, r, tn), lambda l, j: (l, 0, j)),
        out_shape=jax.ShapeDtypeStruct((depth, r, n), F32),
        compiler_params=_cparams(2),
        name="mod_vectors",
    )(cvec, w_mod, b_mod.reshape(depth, 1, n))


def _fourier_prep_kernel(w_ref, c_ref, s_ref, wc_ref, ws_ref):
    w = w_ref[...]
    wc_ref[...] = jnp.dot(c_ref[...], w, preferred_element_type=F32, precision=lax.Precision.HIGHEST)
    ws_ref[...] = jnp.dot(s_ref[...], w, preferred_element_type=F32, precision=lax.Precision.HIGHEST)


def _fourier_prep_call(w_bd, c_bd, s_bd):
    depth, n, _ = w_bd.shape
    spec_w = pl.BlockSpec((None, n, n), lambda l: (l, 0, 0))
    spec_t = pl.BlockSpec((n, n), lambda l: (0, 0))
    return pl.pallas_call(
        _fourier_prep_kernel,
        grid=(depth,),
        in_specs=[spec_w, spec_t, spec_t],
        out_specs=[spec_w, spec_w],
        out_shape=[jax.ShapeDtypeStruct((depth, n, n), F32)] * 2,
        compiler_params=_cparams(1),
        name="fourier_prep",
    )(w_bd, c_bd, s_bd)


def _rope(v, cos, sin_next, sin_prev, shift):
    return (v * cos + pltpu.roll(v, LANES - shift, 1) * sin_next + pltpu.roll(v, shift, 1) * sin_prev)


def _even_odd_variants(v, ones_lane=False):
    lane = lax.broadcasted_iota(jnp.int32, v.shape, 1)
    lo = lane < HEAD_DIM
    vr = pltpu.roll(v, HEAD_DIM, 1)
    pad_lo = jnp.where(lane == 0, 1.0, 0.0) if ones_lane else jnp.zeros_like(v)
    pad_hi = jnp.where(lane == HEAD_DIM, 1.0, 0.0) if ones_lane else jnp.zeros_like(v)
    return (jnp.where(lo, v, pad_hi), jnp.where(lo, pad_lo, vr), jnp.where(lo, vr, pad_hi), jnp.where(lo, pad_lo, v))


def _inproj_kernel(x_ref, mod_ref, g_ref, w_ref, gq_ref, gk_ref, hsum_ref, wcs_ref,
                   ca_ref, san_ref, sap_ref, ct_ref, stn_ref, stp_ref,
                   q_ref, kz_ref, vz_ref, rq_ref, rk_ref, rv_ref, gate_ref, u_ref, *, rope, sub):
    shift = mod_ref[0:1, :]
    scale1 = 1.0 + mod_ref[1:2, :]
    hsum = hsum_ref[...]
    inv_d = 1.0 / HEAD_DIM

    def head_norm(v, gain):
        ssq = jnp.dot((v * v).astype(BF16), hsum, preferred_element_type=F32)
        return v * lax.rsqrt(ssq * inv_d + EPS) * gain

    for r0 in range(0, x_ref.shape[0], sub):
        rows = slice(r0, r0 + sub)
        x = x_ref[rows, :]
        ms = jnp.mean(x * x, axis=-1, keepdims=True)
        a = ((x * lax.rsqrt(ms + EPS)) * g_ref[...]) * scale1 + shift
        pall = jnp.dot(a.astype(BF16), w_ref[...], preferred_element_type=F32)

        def pcols(lo, hi):
            return pall[:, lo:hi]

        if rope:
            att_tab = (ca_ref[rows, :], san_ref[rows, :], sap_ref[rows, :], HEAD_DIM // 4)
            ret_tab = (ct_ref[rows, :], stn_ref[rows, :], stp_ref[rows, :], HEAD_DIM // 2)

        for j in range(ATT_W // 256):
            qn = head_norm(pcols(O_Q + j * 256, O_Q + (j + 1) * 256), gq_ref[...])
            for c in range(2):
                col = qn[:, c * LANES:(c + 1) * LANES]
                if rope:
                    col = _rope(col, *att_tab)
                lo = j * 256 + c * LANES
                q_ref[rows, lo:lo + LANES] = (col * Q_SCALE).astype(BF16)

        k = pcols(O_K, O_V)
        ssq = jnp.dot((k * k).astype(BF16), hsum[0:LANES, 0:LANES], preferred_element_type=F32)
        k = k * lax.rsqrt(ssq * inv_d + EPS) * gk_ref[...]
        if rope:
            k = _rope(k, *att_tab)
        for i, var in enumerate(_even_odd_variants(k)):
            kz_ref[rows, i * LANES:(i + 1) * LANES] = var.astype(BF16)
        for i, var in enumerate(_even_odd_variants(pcols(O_V, O_RQ), ones_lane=True)):
            vz_ref[rows, i * LANES:(i + 1) * LANES] = var.astype(BF16)

        for c in range(RET_W // LANES):
            qc = pcols(O_RQ + c * LANES, O_RQ + (c + 1) * LANES)
            kc = pcols(O_RK + c * LANES, O_RK + (c + 1) * LANES) * HEAD_DIM ** -0.5
            if rope:
                qc = _rope(qc, *ret_tab)
                kc = _rope(kc, *ret_tab)
            rq_ref[rows, c * LANES:(c + 1) * LANES] = qc.astype(BF16)
            rk_ref[rows, c * LANES:(c + 1) * LANES] = kc.astype(BF16)
        rv_ref[rows, :] = pcols(O_RV, O_GF).astype(BF16)
        gate_ref[rows, :] = pcols(O_GF, O_FU).astype(BF16)

        fu = pcols(O_FU, PROJ_W).astype(BF16)
        u_ref[rows, :] = jnp.dot(fu, wcs_ref[...], preferred_element_type=F32).astype(BF16)


def _inproj_call(layer, x, mod, mod_row, g1, w_in, gq, gk, hsum, wcs, tabs, *, rope, tm):
    b, t, d = x.shape
    per_layer = lambda *shape: pl.BlockSpec((None,) + shape, lambda i, j: (layer,) + (0,) * len(shape))
    tab = pl.BlockSpec((tm, LANES), lambda i, j: (j, 0))
    tok = lambda w: pl.BlockSpec((None, tm, w), lambda i, j: (i, j, 0))
    out_w = (ATT_W, 4 * LANES, 4 * LANES, RET_W, RET_W, RET_W, 2 * RET_W, 2 * FOUR_W)
    return pl.pallas_call(
        functools.partial(_inproj_kernel, rope=rope, sub=tm),
        grid=(b, t // tm),
        in_specs=[tok(d),
                  pl.BlockSpec((None, None, N_MOD, d), lambda i, j: (layer, mod_row(i), 0, 0)),
                  per_layer(1, d),
                  per_layer(d, PROJ_W),
                  per_layer(1, 256),
                  per_layer(1, LANES),
                  pl.BlockSpec((256, 256), lambda i, j: (0, 0)),
                  per_layer(FOUR_W, 2 * FOUR_W),
                  tab, tab, tab, tab, tab, tab],
        out_specs=[tok(w) for w in out_w],
        out_shape=[jax.ShapeDtypeStruct((b, t, w), BF16) for w in out_w],
        compiler_params=_cparams(2),
        name="in_projection",
    )(x, mod, g1, w_in, gq, gk, hsum, wcs, *tabs)


def _attn_kernel(sink_ref, aux_ref, bias_ref, q_ref, kz_ref, vz_ref, kcz_ref, vcz_ref, o_ref, *, band, t, tq):
    n = pl.program_id(1)
    contract_last = (((1,), (1,)), ((), ()))
    if band:
        span = tq + 2 * WINDOW
        start = pl.multiple_of(jnp.clip(n * tq - WINDOW, 0, t - span), BLOCK)
        last = t // tq - 1
        bias = bias_ref[jnp.where(n == 0, 0, jnp.where(n == last, 2, 1))]
    upper = lax.broadcasted_iota(jnp.int32, (2 * tq, 1), 0) < tq
    lane_lo = lax.broadcasted_iota(jnp.int32, (2 * tq, LANES), 1) < HEAD_DIM

    def run(use_bound):
        for h in range(ATT_KV_HEADS):
            qq = jnp.concatenate([q_ref[:, (2 * h) * LANES:(2 * h + 1) * LANES],
                                  q_ref[:, (2 * h + 1) * LANES:(2 * h + 2) * LANES]], axis=0)
            scaled = []
            for par in range(2):
                col = (2 * h + par) * LANES
                sink = jnp.where(upper, sink_ref[4 * h + par], sink_ref[4 * h + 2 + par]) * LOG2E
                s_ctx = lax.dot_general(qq, kcz_ref[:, col:col + LANES], contract_last, preferred_element_type=F32)
                if band:
                    s_band = lax.dot_general(qq, kz_ref[pl.ds(start, span), col:col + LANES], contract_last,
                                             preferred_element_type=F32)
                if use_bound:
                    m = jnp.maximum(aux_ref[0], sink)
                    if band:
                        s_band = s_band + (bias - m)
                else:
                    m = jnp.maximum(jnp.max(s_ctx, axis=-1, keepdims=True), sink)
                    if band:
                        s_band = s_band + bias
                        m = jnp.maximum(m, jnp.max(s_band, axis=-1, keepdims=True))
                        s_band = s_band - m
                acc = jnp.dot(jnp.exp2(s_ctx - m).astype(BF16), vcz_ref[:, col:col + LANES],
                              preferred_element_type=F32)
                if band:
                    acc = acc + jnp.dot(jnp.exp2(s_band).astype(BF16), vz_ref[pl.ds(start, span), col:col + LANES],
                                        preferred_element_type=F32)
                ones = HEAD_DIM if par == 0 else 0
                den = acc[:, ones:ones + 1] + jnp.exp2(sink - m)
                scaled.append(acc * (1.0 / den))
            out = jnp.where(lane_lo, scaled[0], scaled[1])
            o_ref[:, (2 * h) * LANES:(2 * h + 1) * LANES] = out[0:tq].astype(BF16)
            o_ref[:, (2 * h + 1) * LANES:(2 * h + 2) * LANES] = out[tq:2 * tq].astype(BF16)

    pl.when(aux_ref[1] > 0.5)(lambda: run(True))
    pl.when(aux_ref[1] <= 0.5)(lambda: run(False))


def _attn_call(sink, aux, bias, q, kz, vz, kcz, vcz, *, band):
    b, t, _ = q.shape
    lc = kcz.shape[1]
    tq = bias.shape[1] // 2
    full = lambda rows: pl.BlockSpec((None, rows, 4 * LANES), lambda i, j: (i, 0, 0))
    blk = pl.BlockSpec((None, tq, ATT_W), lambda i, j: (i, j, 0))
    return pl.pallas_call(
        functools.partial(_attn_kernel, band=band, t=t, tq=tq),
        grid=(b, t // tq),
        in_specs=[pl.BlockSpec(memory_space=pltpu.SMEM), pl.BlockSpec(memory_space=pltpu.SMEM),
                  pl.BlockSpec(bias.shape, lambda i, j: (0, 0, 0)),
                  blk, full(kz.shape[1]), full(vz.shape[1]), full(lc), full(lc)],
        out_specs=blk,
        out_shape=jax.ShapeDtypeStruct((b, t, ATT_W), BF16),
        compiler_params=_cparams(2),
        name="window_attention" if band else "context_attention",
    )(sink, aux, bias, q, kz, vz, kcz, vcz)


def _window_bias(tq):
    span = tq + 2 * WINDOW
    r = np.arange(2 * tq)[:, None] % tq
    c = np.arange(span)[None, :]
    cases = [np.where(np.abs(c - r - off) <= WINDOW, 0.0, NEG_INF) for off in (0, WINDOW, 2 * WINDOW)]
    return jnp.asarray(np.stack(cases), F32)


def _ret_kernel(lg_ref, lgl_ref, gn_ref, havg_ref,
                rq_ref, rk_ref, rv_ref, gate_ref, cq_ref, ck_ref, cv_ref, cgate_ref,
                o_ref, oc_ref,
                dec_ref, st_ref, of_ref, ob_ref, ocf_ref, ocb_ref, *, t, lc):
    c = RET_CHUNK

    @pl.when(pl.program_id(0) == 0)
    def _():
        ri = lax.broadcasted_iota(jnp.int32, (c, c), 0)
        ci = lax.broadcasted_iota(jnp.int32, (c, c), 1)
        diff = (ri - ci).astype(F32)
        for h in range(RET_HEADS):
            dec_ref[0, :, h * c:(h + 1) * c] = jnp.where(ri >= ci, jnp.exp(jnp.maximum(diff, 0.0) * lg_ref[0, h]), 0.0)
            dec_ref[1, :, h * c:(h + 1) * c] = jnp.where(ci >= ri, jnp.exp(jnp.maximum(-diff, 0.0) * lg_ref[1, h]), 0.0)

    idx = lax.broadcasted_iota(jnp.int32, (c, RET_W), 0).astype(F32)
    lgf = lgl_ref[0:1, :]
    lgb = lgl_ref[1:2, :]
    xi = (jnp.exp((idx + 1.0) * lgf), jnp.exp((c - idx) * lgb))
    zeta = (jnp.exp((c - 1.0 - idx) * lgf), jnp.exp(idx * lgb))
    cdec = (jnp.exp(c * lgf), jnp.exp(c * lgb))
    lane_head = lax.broadcasted_iota(jnp.int32, (1, RET_W), 1) // HEAD_DIM
    head_mask = [lane_head == h for h in range(RET_HEADS)]
    blockdiag = (lax.broadcasted_iota(jnp.int32, (RET_W, RET_W), 0) // HEAD_DIM
                 == lax.broadcasted_iota(jnp.int32, (RET_W, RET_W), 1) // HEAD_DIM)
    contract_last = (((1,), (1,)), ((), ()))
    contract_first = (((0,), (0,)), ((), ()))

    def by_head(v):
        zero = jnp.zeros_like(v)
        return jnp.concatenate([jnp.where(head_mask[h], v, zero) for h in range(RET_HEADS)], axis=0)

    st_ref[...] = jnp.zeros_like(st_ref)

    def run(n_chunks, q_ref_, k_ref_, v_ref_, out_refs):
        group = min(4, n_chunks)

        def body(i, carry):
            work = []
            for u in range(group):
                for direction in (0, 1):
                    idx = i * group + u
                    r0 = pl.multiple_of((idx if direction == 0 else n_chunks - 1 - idx) * c, c)
                    q, k, v = q_ref_[pl.ds(r0, c), :], k_ref_[pl.ds(r0, c), :], v_ref_[pl.ds(r0, c), :]
                    s = lax.dot_general(q, by_head(k), contract_last, preferred_element_type=F32)
                    kd = (k.astype(F32) * zeta[direction]).astype(BF16)
                    kv = lax.dot_general(kd, v, contract_first, preferred_element_type=F32)
                    work.append((direction, r0, q, v, s, kv))
            cross = []
            for direction, r0, q, v, s, kv in work:
                st = st_ref[direction]
                cross.append(jnp.dot(q, st.astype(BF16), preferred_element_type=F32) * xi[direction])
                st_ref[direction] = st * cdec[direction] + jnp.where(blockdiag, kv, 0.0)
            for (direction, r0, q, v, s, kv), cr in zip(work, cross):
                p = (s * dec_ref[direction]).astype(BF16)
                out_refs[direction][pl.ds(r0, c), :] = jnp.dot(p, by_head(v), preferred_element_type=F32) + cr
            return carry

        lax.fori_loop(0, n_chunks // group, body, 0)

    run(lc // c, cq_ref, ck_ref, cv_ref, (ocf_ref, ocb_ref))
    run(t // c, rq_ref, rk_ref, rv_ref, (of_ref, ob_ref))

    havg = havg_ref[...]
    gn_g = gn_ref[...]

    def group_norm(o):
        mu = jnp.dot(o.astype(BF16), havg, preferred_element_type=F32)
        d = o - mu
        var = jnp.dot((d * d).astype(BF16), havg, preferred_element_type=F32)
        return d * lax.rsqrt(var + EPS) * gn_g

    def finish(of, ob, gates):
        gf = gates[:, 0:RET_W].astype(F32)
        gb = gates[:, RET_W:2 * RET_W].astype(F32)
        return (group_norm(of) * _silu(gf) + group_norm(ob) * _silu(gb)).astype(BF16)

    o_ref[...] = finish(of_ref[...], ob_ref[...], gate_ref[...])
    oc_ref[...] = finish(ocf_ref[...], ocb_ref[...], cgate_ref[...])


def _ret_call(lg, lgl, gn_g, havg, rq, rk, rv, gate, cq, ck, cv, cgate):
    b, t, _ = rq.shape
    lc = cq.shape[1]
    row = lambda i: (0, 0)
    seq = lambda rows, w: pl.BlockSpec((None, rows, w), lambda i: (i, 0, 0))
    return pl.pallas_call(
        functools.partial(_ret_kernel, t=t, lc=lc),
        grid=(b,),
        in_specs=[pl.BlockSpec(memory_space=pltpu.SMEM),
                  pl.BlockSpec((2, RET_W), row), pl.BlockSpec((1, RET_W), row), pl.BlockSpec((RET_W, RET_W), row),
                  seq(t, RET_W), seq(t, RET_W), seq(t, RET_W), seq(t, 2 * RET_W),
                  seq(lc, RET_W), seq(lc, RET_W), seq(lc, RET_W), seq(lc, 2 * RET_W)],
        out_specs=[seq(t, RET_W), seq(lc, RET_W)],
        out_shape=[jax.ShapeDtypeStruct((b, t, RET_W), BF16), jax.ShapeDtypeStruct((b, lc, RET_W), BF16)],
        scratch_shapes=[pltpu.VMEM((2, RET_CHUNK, RET_HEADS * RET_CHUNK), F32),
                        pltpu.VMEM((2, RET_W, RET_W), F32),
                        pltpu.VMEM((t, RET_W), F32), pltpu.VMEM((t, RET_W), F32),
                        pltpu.VMEM((lc, RET_W), F32), pltpu.VMEM((lc, RET_W), F32)],
        compiler_params=_cparams(1),
        name="retention",
    )(lg, lgl, gn_g, havg, rq, rk, rv, gate, cq, ck, cv, cgate)


def _fourier_kernel(c_ref, s_ref, flip_ref, u_ref, o_ref, *, t, blk):
    half = t // 2
    a = jnp.dot(c_ref[...], u_ref[:, 0:FOUR_W], preferred_element_type=F32)
    bs = jnp.dot(s_ref[...], u_ref[:, FOUR_W:2 * FOUR_W], preferred_element_type=F32)
    o_ref[0:half, :] = (a[0:half] + bs[0:half]).astype(BF16)
    mid = (a[half:half + 1] + bs[half:half + 1]).astype(BF16)
    mirror = (a[0:half] - bs[0:half]).astype(BF16)
    row0 = lax.broadcasted_iota(jnp.int32, (blk, FOUR_W), 0) == 0
    n_blk = half // blk
    for i in range(n_blk):
        src = mirror[(n_blk - 1 - i) * blk:(n_blk - i) * blk]
        rev = jnp.dot(flip_ref[...], src, preferred_element_type=F32).astype(BF16)
        first = mid if i == 0 else mirror[(n_blk - i) * blk:(n_blk - i) * blk + 1]
        o_ref[half + i * blk:half + (i + 1) * blk, :] = jnp.where(row0, first, rev)


def _fourier_call(cmat, smat_neg, flip, u):
    b, t, _ = u.shape
    rows = cmat.shape[0]
    blk = flip.shape[0]
    const = lambda *shape: pl.BlockSpec(shape, lambda i: (0,) * len(shape))
    return pl.pallas_call(
        functools.partial(_fourier_kernel, t=t, blk=blk),
        grid=(b,),
        in_specs=[const(rows, t), const(rows, t), const(blk, blk),
                  pl.BlockSpec((None, t, 2 * FOUR_W), lambda i: (i, 0, 0))],
        out_specs=pl.BlockSpec((None, t, FOUR_W), lambda i: (i, 0, 0)),
        out_shape=jax.ShapeDtypeStruct((b, t, FOUR_W), BF16),
        compiler_params=_cparams(1),
        name="fourier_mix",
    )(cmat, smat_neg, flip, u)


def _mlp_kernel(x_ref, mod_ref, g_ref, att_ref, ret_ref, four_ref, wo_ref, w1_ref, w2_ref, o_ref, *, ff_chunk):
    m = (jnp.dot(att_ref[...], wo_ref[0:ATT_W, :], preferred_element_type=F32)
         + jnp.dot(ret_ref[...], wo_ref[ATT_W:ATT_W + RET_W, :], preferred_element_type=F32)
         + jnp.dot(four_ref[...], wo_ref[ATT_W + RET_W:, :], preferred_element_type=F32))
    x1 = x_ref[...] + mod_ref[2:3, :] * m
    ms = jnp.mean(x1 * x1, axis=-1, keepdims=True)
    hn = ((x1 * lax.rsqrt(ms + EPS)) * g_ref[...]) * (1.0 + mod_ref[4:5, :]) + mod_ref[3:4, :]
    hb = hn.astype(BF16)
    d_ff = w1_ref.shape[1]
    acc = jnp.zeros(x1.shape, F32)
    for c in range(d_ff // ff_chunk):
        hid = jnp.dot(hb, w1_ref[:, c * ff_chunk:(c + 1) * ff_chunk], preferred_element_type=F32)
        hid = jnp.square(jnp.maximum(hid, 0.0)).astype(BF16)
        acc = acc + jnp.dot(hid, w2_ref[c * ff_chunk:(c + 1) * ff_chunk, :], preferred_element_type=F32)
    o_ref[...] = x1 + mod_ref[5:6, :] * acc


def _mlp_call(layer, x, mod, mod_row, g2, att, ret, four, w_out, w1, w2, *, tm):
    b, t, d = x.shape
    d_ff = w1.shape[-1]
    tok = lambda w: pl.BlockSpec((None, tm, w), lambda i, j: (i, j, 0))
    once = dict(pipeline_mode=pl.Buffered(1))
    per_layer = lambda *shape, **kw: pl.BlockSpec((None,) + shape, lambda i, j: (layer,) + (0,) * len(shape), **kw)
    return pl.pallas_call(
        functools.partial(_mlp_kernel, ff_chunk=1024),
        grid=(b, t // tm),
        in_specs=[tok(d),
                  pl.BlockSpec((None, None, N_MOD, d), lambda i, j: (layer, mod_row(i), 0, 0)),
                  per_layer(1, d),
                  tok(ATT_W), tok(RET_W), tok(FOUR_W),
                  per_layer(d, d, **once),
                  per_layer(d, d_ff, **once),
                  per_layer(d_ff, d, **once)],
        out_specs=tok(d),
        out_shape=jax.ShapeDtypeStruct((b, t, d), F32),
        compiler_params=_cparams(2),
        name="out_projection_mlp",
    )(x, mod, g2, att, ret, four, w_out, w1, w2)


def _rope_tables(pos_groups, half):
    freqs = ROPE_BASE ** (-jnp.arange(half, dtype=F32) / half)
    cos, sin_next, sin_prev = [], [], []
    zeros = None
    for pos in pos_groups:
        ang = pos.astype(F32)[:, None] * freqs[None, :]
        c, s = jnp.cos(ang), jnp.sin(ang)
        zeros = jnp.zeros_like(s)
        cos += [c, c]
        sin_next += [-s, zeros]
        sin_prev += [zeros, s]
    reps = LANES // (2 * half * len(pos_groups))
    cat = lambda parts: jnp.tile(jnp.concatenate(parts, axis=1), (1, reps))
    return cat(cos), cat(sin_next), cat(sin_prev)


def _dft_tables(n):
    k = np.arange(n, dtype=np.int64)
    ang = 2.0 * np.pi * ((k[:, None] * k[None, :]) % n).astype(np.float64) / n
    return np.cos(ang), np.sin(ang)


def _block_diag(blocks):
    n = blocks.shape[-1]
    g = blocks.shape[-3]
    eye = jnp.eye(g, dtype=blocks.dtype)
    out = blocks[..., :, :, None, :] * eye[:, None, :, None]
    return out.reshape(blocks.shape[:-3] + (g * n, g * n))


def kernel(x, c, ctx, c_ctx, w_mod, b_mod, norm1_g, norm2_g, w_in, w_out, q_norm_g, k_norm_g, attn_sink,
           ret_decay_logit, ret_gn_g, fourier_w, w_ff1, w_ff2):
    b, s, d = x.shape
    lc = ctx.shape[1]
    depth = w_mod.shape[0]
    fdim = fourier_w.shape[-1]

    n_rows = -(-(b + 1) // 16) * 16
    cvec = jnp.zeros((n_rows, d), F32).at[:b].set(c).at[b].set(c_ctx)
    mod = _mod_call(cvec, w_mod, b_mod).reshape(depth, n_rows, N_MOD, d)

    pos = jnp.arange(s)
    tabs_lat = _rope_tables([pos // GRID_W, pos % GRID_W], HEAD_DIM // 4) + _rope_tables([pos], HEAD_DIM // 2)
    tabs_ctx = tuple(jnp.zeros((lc, LANES), F32) for _ in range(6))
    head_id = np.arange(256) // HEAD_DIM
    hsum = jnp.asarray(head_id[:, None] == head_id[None, :], BF16)
    havg = jnp.asarray((head_id[:, None] == head_id[None, :]) / HEAD_DIM, BF16)
    c_ch, s_ch = _dft_tables(fdim)
    eye_g = np.eye(FOURIER_GROUPS)
    c_bd = jnp.asarray(np.kron(eye_g, c_ch), F32)
    s_bd = jnp.asarray(np.kron(eye_g, s_ch), F32)

    def pos_tables(n):
        cn, sn = _dft_tables(n)
        norm = 1.0 / np.sqrt(float(n) * fdim)
        rows = n // 2 + 16
        blk = min(256, n // 2)
        r = np.arange(blk)
        flip = ((r[:, None] + r[None, :]) == blk).astype(np.float32)
        return (jnp.asarray(cn[:rows] * norm, F32).astype(BF16), jnp.asarray(-sn[:rows] * norm, F32).astype(BF16),
                jnp.asarray(flip, BF16))

    ftab_s = pos_tables(s)
    ftab_c = pos_tables(lc)
    win_bias = _window_bias(2 * BLOCK)
    logit_bound = (HEAD_DIM * Q_SCALE * 1.02) * jnp.max(jnp.abs(q_norm_g), axis=-1) * jnp.max(jnp.abs(k_norm_g), axis=-1)
    spread = logit_bound + jnp.maximum(logit_bound, jnp.max(attn_sink, axis=-1) * LOG2E)
    attn_aux = jnp.stack([logit_bound, (spread < 100.0).astype(F32)], axis=-1).astype(F32)

    wc, ws = _fourier_prep_call(_block_diag(fourier_w.astype(F32)), c_bd, s_bd)
    wcs = jnp.concatenate([wc, ws], axis=-1).astype(BF16)

    w_in_b = w_in.astype(BF16)
    w_out_b = w_out.astype(BF16)
    w1_b = w_ff1.astype(BF16)
    w2_b = w_ff2.astype(BF16)
    g1 = norm1_g.reshape(depth, 1, d)
    g2 = norm2_g.reshape(depth, 1, d)
    gq = jnp.tile(q_norm_g, (1, 256 // HEAD_DIM)).reshape(depth, 1, 256)
    gk = jnp.tile(k_norm_g, (1, LANES // HEAD_DIM)).reshape(depth, 1, LANES)
    log_gamma = -jax.nn.softplus(-ret_decay_logit.astype(F32))
    log_gamma_lanes = jnp.repeat(log_gamma, HEAD_DIM, axis=-1)

    lat_row = lambda i: i
    ctx_row = lambda i: b

    h = ctx
    for l in range(depth):
        need_ctx = l < depth - 1
        common = (g1, w_in_b, gq, gk, hsum, wcs)
        q, kz, vz, rq, rk, rv, gate, u = _inproj_call(l, x, mod, lat_row, *common, tabs_lat, rope=True, tm=512)
        cq, ckz, cvz, crq, crk, crv, cgate, cu = _inproj_call(l, h, mod, ctx_row, *common, tabs_ctx, rope=False, tm=lc)

        att = _attn_call(attn_sink[l], attn_aux[l], win_bias, q, kz, vz, ckz, cvz, band=True)
        ret, ret_c = _ret_call(log_gamma[l], log_gamma_lanes[l], ret_gn_g[l].reshape(1, RET_W), havg,
                               rq, rk, rv, gate, crq, crk, crv, cgate)
        four = _fourier_call(*ftab_s, u)
        x = _mlp_call(l, x, mod, lat_row, g2, att, ret, four, w_out_b, w1_b, w2_b, tm=512)
        if need_ctx:
            att_c = _attn_call(attn_sink[l], attn_aux[l], win_bias, cq, ckz, cvz, ckz, cvz, band=False)
            four_c = _fourier_call(*ftab_c, cu)
            h = _mlp_call(l, h, mod, ctx_row, g2, att_c, ret_c, four_c, w_out_b, w1_b, w2_b, tm=lc)
    return x
```

```python
import functools

import numpy as np
import jax
import jax.numpy as jnp
from jax import lax
from jax.experimental import pallas as pl
from jax.experimental.pallas import tpu as pltpu

F32 = jnp.float32
BF16 = jnp.bfloat16

HEAD_DIM = 64
LANES = 128
GRID_W = 64
N_MOD = 6
ATT_Q_HEADS = 8
ATT_KV_HEADS = 2
WINDOW = 128
BLOCK = 128
RET_HEADS = 4
RET_CHUNK = 128
FOURIER_GROUPS = 4
ROPE_BASE = 10000.0
EPS = 1e-6
NEG_INF = -1e30
VMEM_LIMIT_BYTES = 56 * 1024 * 1024

ATT_W = ATT_Q_HEADS * HEAD_DIM
KV_W = ATT_KV_HEADS * HEAD_DIM
RET_W = RET_HEADS * HEAD_DIM
FOUR_W = 256
O_Q, O_K, O_V = 0, ATT_W, ATT_W + KV_W
O_RQ = ATT_W + 2 * KV_W
O_RK, O_RV, O_GF, O_GB = O_RQ + RET_W, O_RQ + 2 * RET_W, O_RQ + 3 * RET_W, O_RQ + 4 * RET_W
O_FU = O_RQ + 5 * RET_W
PROJ_W = O_FU + FOUR_W
LOG2E = 1.4426950408889634
Q_SCALE = HEAD_DIM ** -0.5 * LOG2E


def _cparams(n_axes, flags=None):
    return pltpu.CompilerParams(dimension_semantics=("arbitrary",) * n_axes,
                                vmem_limit_bytes=VMEM_LIMIT_BYTES, flags=flags)


def _silu(v):
    return v / (1.0 + jnp.exp(-v))


def _mod_kernel(c_ref, w_ref, b_ref, o_ref):
    s = _silu(c_ref[...]).astype(BF16)
    o_ref[...] = jnp.dot(s, w_ref[...].astype(BF16), preferred_element_type=F32) + b_ref[...]


def _mod_call(cvec, w_mod, b_mod):
    depth, d, n = w_mod.shape
    r = cvec.shape[0]
    tn = 1536
    return pl.pallas_call(
        _mod_kernel,
        grid=(depth, n // tn),
        in_specs=[pl.BlockSpec((r, d), lambda l, j: (0, 0)),
                  pl.BlockSpec((None, d, tn), lambda l, j: (l, 0, j)),
                  pl.BlockSpec((None, 1, tn), lambda l, j: (l, 0, j))],
        out_specs=pl.BlockSpec((None, r, tn), lambda l, j: (l, 0, j)),
        out_shape=jax.ShapeDtypeStruct((depth, r, n), F32),
        compiler_params=_cparams(2),
        name="mod_vectors",
    )(cvec, w_mod, b_mod.reshape(depth, 1, n))


def _fourier_prep_kernel(w_ref, c_ref, s_ref, wc_ref, ws_ref):
    w = w_ref[...]
    wc_ref[...] = jnp.dot(c_ref[...], w, preferred_element_type=F32, precision=lax.Precision.HIGHEST)
    ws_ref[...] = jnp.dot(s_ref[...], w, preferred_element_type=F32, precision=lax.Precision.HIGHEST)


def _fourier_prep_call(w_bd, c_bd, s_bd):
    depth, n, _ = w_bd.shape
    spec_w = pl.BlockSpec((None, n, n), lambda l: (l, 0, 0))
    spec_t = pl.BlockSpec((n, n), lambda l: (0, 0))
    return pl.pallas_call(
        _fourier_prep_kernel,
        grid=(depth,),
        in_specs=[spec_w, spec_t, spec_t],
        out_specs=[spec_w, spec_w],
        out_shape=[jax.ShapeDtypeStruct((depth, n, n), F32)] * 2,
        compiler_params=_cparams(1),
        name="fourier_prep",
    )(w_bd, c_bd, s_bd)


def _rope(v, cos, sin_next, sin_prev, shift):
    return (v * cos + pltpu.roll(v, LANES - shift, 1) * sin_next + pltpu.roll(v, shift, 1) * sin_prev)


def _even_odd_variants(v, ones_lane=False):
    lane = lax.broadcasted_iota(jnp.int32, v.shape, 1)
    lo = lane < HEAD_DIM
    vr = pltpu.roll(v, HEAD_DIM, 1)
    pad_lo = jnp.where(lane == 0, 1.0, 0.0) if ones_lane else jnp.zeros_like(v)
    pad_hi = jnp.where(lane == HEAD_DIM, 1.0, 0.0) if ones_lane else jnp.zeros_like(v)
    return (jnp.where(lo, v, pad_hi), jnp.where(lo, pad_lo, vr), jnp.where(lo, vr, pad_hi), jnp.where(lo, pad_lo, v))


def _inproj_kernel(x_ref, mod_ref, g_ref, w_ref, gq_ref, gk_ref, hsum_ref, wcs_ref,
                   ca_ref, san_ref, sap_ref, ct_ref, stn_ref, stp_ref,
                   q_ref, kz_ref, vz_ref, rq_ref, rk_ref, rv_ref, gate_ref, u_ref, *, rope):
    hsum = hsum_ref[...]
    inv_d = 1.0 / HEAD_DIM

    def head_norm(v, gain):
        ssq = jnp.dot((v * v).astype(BF16), hsum, preferred_element_type=F32)
        return v * lax.rsqrt(ssq * inv_d + EPS) * gain

    x = x_ref[...]
    ms = jnp.mean(x * x, axis=-1, keepdims=True)
    a = ((x * lax.rsqrt(ms + EPS)) * g_ref[...]) * (1.0 + mod_ref[1:2, :]) + mod_ref[0:1, :]
    proj = jnp.dot(a.astype(BF16), w_ref[...], preferred_element_type=F32)
    if rope:
        att_tab = (ca_ref[...], san_ref[...], sap_ref[...], HEAD_DIM // 4)
        ret_tab = (ct_ref[...], stn_ref[...], stp_ref[...], HEAD_DIM // 2)

    for j in range(ATT_W // 256):
        qn = head_norm(proj[:, O_Q + j * 256:O_Q + (j + 1) * 256], gq_ref[...])
        for c in range(2):
            col = qn[:, c * LANES:(c + 1) * LANES]
            if rope:
                col = _rope(col, *att_tab)
            lo = j * 256 + c * LANES
            q_ref[:, lo:lo + LANES] = (col * Q_SCALE).astype(BF16)

    k = proj[:, O_K:O_V]
    ssq = jnp.dot((k * k).astype(BF16), hsum[0:LANES, 0:LANES], preferred_element_type=F32)
    k = k * lax.rsqrt(ssq * inv_d + EPS) * gk_ref[...]
    if rope:
        k = _rope(k, *att_tab)
    for i, var in enumerate(_even_odd_variants(k)):
        kz_ref[:, i * LANES:(i + 1) * LANES] = var.astype(BF16)
    for i, var in enumerate(_even_odd_variants(proj[:, O_V:O_RQ], ones_lane=True)):
        vz_ref[:, i * LANES:(i + 1) * LANES] = var.astype(BF16)

    for c in range(RET_W // LANES):
        qc = proj[:, O_RQ + c * LANES:O_RQ + (c + 1) * LANES]
        kc = proj[:, O_RK + c * LANES:O_RK + (c + 1) * LANES] * HEAD_DIM ** -0.5
        if rope:
            qc = _rope(qc, *ret_tab)
            kc = _rope(kc, *ret_tab)
        rq_ref[:, c * LANES:(c + 1) * LANES] = qc.astype(BF16)
        rk_ref[:, c * LANES:(c + 1) * LANES] = kc.astype(BF16)
    rv_ref[...] = proj[:, O_RV:O_GF].astype(BF16)
    gate_ref[...] = proj[:, O_GF:O_FU].astype(BF16)

    fu = proj[:, O_FU:PROJ_W].astype(BF16)
    u_ref[...] = jnp.dot(fu, wcs_ref[...], preferred_element_type=F32).astype(BF16)


def _inproj_call(layer, x, mod, mod_row, g1, w_in, gq, gk, hsum, wcs, tabs, *, rope, tm):
    b, t, d = x.shape
    per_layer = lambda *shape: pl.BlockSpec((None,) + shape, lambda i, j: (layer,) + (0,) * len(shape))
    tab = pl.BlockSpec((tm, LANES), lambda i, j: (j, 0))
    tok = lambda w: pl.BlockSpec((None, tm, w), lambda i, j: (i, j, 0))
    out_w = (ATT_W, 4 * LANES, 4 * LANES, RET_W, RET_W, RET_W, 2 * RET_W, 2 * FOUR_W)
    return pl.pallas_call(
        functools.partial(_inproj_kernel, rope=rope),
        grid=(b, t // tm),
        in_specs=[tok(d),
                  pl.BlockSpec((None, None, N_MOD, d), lambda i, j: (layer, mod_row(i), 0, 0)),
                  per_layer(1, d),
                  per_layer(d, PROJ_W),
                  per_layer(1, 256),
                  per_layer(1, LANES),
                  pl.BlockSpec((256, 256), lambda i, j: (0, 0)),
                  per_layer(FOUR_W, 2 * FOUR_W),
                  tab, tab, tab, tab, tab, tab],
        out_specs=[tok(w) for w in out_w],
        out_shape=[jax.ShapeDtypeStruct((b, t, w), BF16) for w in out_w],
        compiler_params=_cparams(2),
        name="in_projection",
    )(x, mod, g1, w_in, gq, gk, hsum, wcs, *tabs)


def _attn_kernel(sink_ref, aux_ref, bias_ref, q_ref, kz_ref, vz_ref, kcz_ref, vcz_ref, o_ref, *, band, t, tq, tiles):
    contract_last = (((1,), (1,)), ((), ()))
    span = tq + 2 * WINDOW
    starts, biases = [], []
    if band:
        last = t // tq - 1
        for sub in range(tiles):
            n = pl.program_id(1) * tiles + sub
            starts.append(pl.multiple_of(jnp.clip(n * tq - WINDOW, 0, t - span), BLOCK))
            biases.append(bias_ref[jnp.where(n == 0, 0, jnp.where(n == last, 2, 1))])
    upper = lax.broadcasted_iota(jnp.int32, (2 * tq, 1), 0) < tq
    lane_lo = lax.broadcasted_iota(jnp.int32, (2 * tq, LANES), 1) < HEAD_DIM
    units = [(sub, h) for sub in range(tiles) for h in range(ATT_KV_HEADS)]

    def run(use_bound):
        scores = {}
        for sub, h in units:
            rows = slice(sub * tq, (sub + 1) * tq)
            qq = jnp.concatenate([q_ref[rows, (2 * h) * LANES:(2 * h + 1) * LANES],
                                  q_ref[rows, (2 * h + 1) * LANES:(2 * h + 2) * LANES]], axis=0)
            for par in range(2):
                col = (2 * h + par) * LANES
                s_ctx = lax.dot_general(qq, kcz_ref[:, col:col + LANES], contract_last, preferred_element_type=F32)
                s_band = None
                if band:
                    s_band = lax.dot_general(qq, kz_ref[pl.ds(starts[sub], span), col:col + LANES], contract_last,
                                             preferred_element_type=F32)
                scores[sub, h, par] = (s_ctx, s_band)
        for sub, h in units:
            rows = slice(sub * tq, (sub + 1) * tq)
            scaled = []
            for par in range(2):
                col = (2 * h + par) * LANES
                s_ctx, s_band = scores[sub, h, par]
                sink = jnp.where(upper, sink_ref[4 * h + par], sink_ref[4 * h + 2 + par]) * LOG2E
                if use_bound:
                    m = jnp.maximum(aux_ref[0], sink)
                    if band:
                        s_band = s_band + (biases[sub] - m)
                else:
                    m = jnp.maximum(jnp.max(s_ctx, axis=-1, keepdims=True), sink)
                    if band:
                        s_band = s_band + biases[sub]
                        m = jnp.maximum(m, jnp.max(s_band, axis=-1, keepdims=True))
                        s_band = s_band - m
                acc = jnp.dot(jnp.exp2(s_ctx - m).astype(BF16), vcz_ref[:, col:col + LANES],
                              preferred_element_type=F32)
                if band:
                    acc = acc + jnp.dot(jnp.exp2(s_band).astype(BF16),
                                        vz_ref[pl.ds(starts[sub], span), col:col + LANES], preferred_element_type=F32)
                ones = HEAD_DIM if par == 0 else 0
                den = acc[:, ones:ones + 1] + jnp.exp2(sink - m)
                scaled.append(acc * (1.0 / den))
            out = jnp.where(lane_lo, scaled[0], scaled[1])
            o_ref[rows, (2 * h) * LANES:(2 * h + 1) * LANES] = out[0:tq].astype(BF16)
            o_ref[rows, (2 * h + 1) * LANES:(2 * h + 2) * LANES] = out[tq:2 * tq].astype(BF16)

    pl.when(aux_ref[1] > 0.5)(lambda: run(True))
    pl.when(aux_ref[1] <= 0.5)(lambda: run(False))


def _attn_call(sink, aux, bias, q, kz, vz, kcz, vcz, *, band):
    b, t, _ = q.shape
    lc = kcz.shape[1]
    tq = bias.shape[1] // 2
    tiles = 2 if t % (2 * tq) == 0 else 1
    full = lambda rows: pl.BlockSpec((None, rows, 4 * LANES), lambda i, j: (i, 0, 0))
    blk = pl.BlockSpec((None, tiles * tq, ATT_W), lambda i, j: (i, j, 0))
    return pl.pallas_call(
        functools.partial(_attn_kernel, band=band, t=t, tq=tq, tiles=tiles),
        grid=(b, t // (tiles * tq)),
        in_specs=[pl.BlockSpec(memory_space=pltpu.SMEM), pl.BlockSpec(memory_space=pltpu.SMEM),
                  pl.BlockSpec(bias.shape, lambda i, j: (0, 0, 0)),
                  blk, full(kz.shape[1]), full(vz.shape[1]), full(lc), full(lc)],
        out_specs=blk,
        out_shape=jax.ShapeDtypeStruct((b, t, ATT_W), BF16),
        compiler_params=_cparams(2),
        name="window_attention" if band else "context_attention",
    )(sink, aux, bias, q, kz, vz, kcz, vcz)


def _window_bias(tq):
    span = tq + 2 * WINDOW
    r = np.arange(2 * tq)[:, None] % tq
    c = np.arange(span)[None, :]
    cases = [np.where(np.abs(c - r - off) <= WINDOW, 0.0, NEG_INF) for off in (0, WINDOW, 2 * WINDOW)]
    return jnp.asarray(np.stack(cases), F32)


def _ret_kernel(lg_ref, lgl_ref, gn_ref, havg_ref,
                rq_ref, rk_ref, rv_ref, gate_ref, cq_ref, ck_ref, cv_ref, cgate_ref,
                o_ref, oc_ref,
                dec_ref, st_ref, of_ref, ob_ref, ocf_ref, ocb_ref, *, t, lc):
    c = RET_CHUNK

    @pl.when(pl.program_id(0) == 0)
    def _():
        ri = lax.broadcasted_iota(jnp.int32, (c, c), 0)
        ci = lax.broadcasted_iota(jnp.int32, (c, c), 1)
        diff = (ri - ci).astype(F32)
        for h in range(RET_HEADS):
            dec_ref[0, :, h * c:(h + 1) * c] = jnp.where(ri >= ci, jnp.exp(jnp.maximum(diff, 0.0) * lg_ref[0, h]), 0.0)
            dec_ref[1, :, h * c:(h + 1) * c] = jnp.where(ci >= ri, jnp.exp(jnp.maximum(-diff, 0.0) * lg_ref[1, h]), 0.0)

    idx = lax.broadcasted_iota(jnp.int32, (c, RET_W), 0).astype(F32)
    lgf = lgl_ref[0:1, :]
    lgb = lgl_ref[1:2, :]
    xi = (jnp.exp((idx + 1.0) * lgf), jnp.exp((c - idx) * lgb))
    zeta = (jnp.exp((c - 1.0 - idx) * lgf), jnp.exp(idx * lgb))
    cdec = (jnp.exp(c * lgf), jnp.exp(c * lgb))
    lane_head = lax.broadcasted_iota(jnp.int32, (1, RET_W), 1) // HEAD_DIM
    head_mask = [lane_head == h for h in range(RET_HEADS)]
    blockdiag = (lax.broadcasted_iota(jnp.int32, (RET_W, RET_W), 0) // HEAD_DIM
                 == lax.broadcasted_iota(jnp.int32, (RET_W, RET_W), 1) // HEAD_DIM)
    contract_last = (((1,), (1,)), ((), ()))
    contract_first = (((0,), (0,)), ((), ()))

    def by_head(v):
        zero = jnp.zeros_like(v)
        return jnp.concatenate([jnp.where(head_mask[h], v, zero) for h in range(RET_HEADS)], axis=0)

    st_ref[...] = jnp.zeros_like(st_ref)

    def run(n_chunks, q_ref_, k_ref_, v_ref_, out_refs):
        group = min(4, n_chunks)

        def body(i, carry):
            work = []
            for u in range(group):
                for direction in (0, 1):
                    idx = i * group + u
                    r0 = pl.multiple_of((idx if direction == 0 else n_chunks - 1 - idx) * c, c)
                    q, k, v = q_ref_[pl.ds(r0, c), :], k_ref_[pl.ds(r0, c), :], v_ref_[pl.ds(r0, c), :]
                    s = lax.dot_general(q, by_head(k), contract_last, preferred_element_type=F32)
                    kd = (k.astype(F32) * zeta[direction]).astype(BF16)
                    kv = lax.dot_general(kd, v, contract_first, preferred_element_type=F32)
                    work.append((direction, r0, q, v, s, kv))
            cross = []
            for direction, r0, q, v, s, kv in work:
                st = st_ref[direction]
                cross.append(jnp.dot(q, st.astype(BF16), preferred_element_type=F32) * xi[direction])
                st_ref[direction] = st * cdec[direction] + jnp.where(blockdiag, kv, 0.0)
            for (direction, r0, q, v, s, kv), cr in zip(work, cross):
                p = (s * dec_ref[direction]).astype(BF16)
                out_refs[direction][pl.ds(r0, c), :] = jnp.dot(p, by_head(v), preferred_element_type=F32) + cr
            return carry

        lax.fori_loop(0, n_chunks // group, body, 0)

    run(lc // c, cq_ref, ck_ref, cv_ref, (ocf_ref, ocb_ref))
    run(t // c, rq_ref, rk_ref, rv_ref, (of_ref, ob_ref))

    havg = havg_ref[...]
    gn_g = gn_ref[...]

    def standardize(o):
        mu = jnp.dot(o.astype(BF16), havg, preferred_element_type=F32)
        d = o - mu
        var = jnp.dot((d * d).astype(BF16), havg, preferred_element_type=F32)
        return d * lax.rsqrt(var + EPS)

    def finish(of, ob, gates):
        gf = gates[:, 0:RET_W].astype(F32)
        gb = gates[:, RET_W:2 * RET_W].astype(F32)
        return ((standardize(of) * _silu(gf) + standardize(ob) * _silu(gb)) * gn_g).astype(BF16)

    o_ref[...] = finish(of_ref[...], ob_ref[...], gate_ref[...])
    oc_ref[...] = finish(ocf_ref[...], ocb_ref[...], cgate_ref[...])


def _ret_call(lg, lgl, gn_g, havg, rq, rk, rv, gate, cq, ck, cv, cgate):
    b, t, _ = rq.shape
    lc = cq.shape[1]
    row = lambda i: (0, 0)
    seq = lambda rows, w: pl.BlockSpec((None, rows, w), lambda i: (i, 0, 0))
    return pl.pallas_call(
        functools.partial(_ret_kernel, t=t, lc=lc),
        grid=(b,),
        in_specs=[pl.BlockSpec(memory_space=pltpu.SMEM),
                  pl.BlockSpec((2, RET_W), row), pl.BlockSpec((1, RET_W), row), pl.BlockSpec((RET_W, RET_W), row),
                  seq(t, RET_W), seq(t, RET_W), seq(t, RET_W), seq(t, 2 * RET_W),
                  seq(lc, RET_W), seq(lc, RET_W), seq(lc, RET_W), seq(lc, 2 * RET_W)],
        out_specs=[seq(t, RET_W), seq(lc, RET_W)],
        out_shape=[jax.ShapeDtypeStruct((b, t, RET_W), BF16), jax.ShapeDtypeStruct((b, lc, RET_W), BF16)],
        scratch_shapes=[pltpu.VMEM((2, RET_CHUNK, RET_HEADS * RET_CHUNK), F32),
                        pltpu.VMEM((2, RET_W, RET_W), F32),
                        pltpu.VMEM((t, RET_W), F32), pltpu.VMEM((t, RET_W), F32),
                        pltpu.VMEM((lc, RET_W), F32), pltpu.VMEM((lc, RET_W), F32)],
        compiler_params=_cparams(1),
        name="retention",
    )(lg, lgl, gn_g, havg, rq, rk, rv, gate, cq, ck, cv, cgate)


def _fourier_kernel(c_ref, s_ref, flip_ref, u_ref, o_ref, *, t, blk):
    half = t // 2
    a = jnp.dot(c_ref[...], u_ref[:, 0:FOUR_W], preferred_element_type=F32)
    bs = jnp.dot(s_ref[...], u_ref[:, FOUR_W:2 * FOUR_W], preferred_element_type=F32)
    o_ref[0:half, :] = (a[0:half] + bs[0:half]).astype(BF16)
    mid = (a[half:half + 1] + bs[half:half + 1]).astype(BF16)
    mirror = (a[0:half] - bs[0:half]).astype(BF16)
    row0 = lax.broadcasted_iota(jnp.int32, (blk, FOUR_W), 0) == 0
    n_blk = half // blk
    for i in range(n_blk):
        src = mirror[(n_blk - 1 - i) * blk:(n_blk - i) * blk]
        rev = jnp.dot(flip_ref[...], src, preferred_element_type=F32).astype(BF16)
        first = mid if i == 0 else mirror[(n_blk - i) * blk:(n_blk - i) * blk + 1]
        o_ref[half + i * blk:half + (i + 1) * blk, :] = jnp.where(row0, first, rev)


def _fourier_call(cmat, smat_neg, flip, u):
    b, t, _ = u.shape
    rows = cmat.shape[0]
    blk = flip.shape[0]
    const = lambda *shape: pl.BlockSpec(shape, lambda i: (0,) * len(shape))
    return pl.pallas_call(
        functools.partial(_fourier_kernel, t=t, blk=blk),
        grid=(b,),
        in_specs=[const(rows, t), const(rows, t), const(blk, blk),
                  pl.BlockSpec((None, t, 2 * FOUR_W), lambda i: (i, 0, 0))],
        out_specs=pl.BlockSpec((None, t, FOUR_W), lambda i: (i, 0, 0)),
        out_shape=jax.ShapeDtypeStruct((b, t, FOUR_W), BF16),
        compiler_params=_cparams(1),
        name="fourier_mix",
    )(cmat, smat_neg, flip, u)


def _mlp_kernel(x_ref, mod_ref, g_ref, att_ref, ret_ref, four_ref, wo_ref, w1_ref, w2_ref, o_ref, *, ff_chunk):
    m = (jnp.dot(att_ref[...], wo_ref[0:ATT_W, :], preferred_element_type=F32)
         + jnp.dot(ret_ref[...], wo_ref[ATT_W:ATT_W + RET_W, :], preferred_element_type=F32)
         + jnp.dot(four_ref[...], wo_ref[ATT_W + RET_W:, :], preferred_element_type=F32))
    x1 = x_ref[...] + mod_ref[2:3, :] * m
    ms = jnp.mean(x1 * x1, axis=-1, keepdims=True)
    hn = ((x1 * lax.rsqrt(ms + EPS)) * g_ref[...]) * (1.0 + mod_ref[4:5, :]) + mod_ref[3:4, :]
    hb = hn.astype(BF16)
    d_ff = w1_ref.shape[1]
    acc = jnp.zeros(x1.shape, F32)
    for c in range(d_ff // ff_chunk):
        hid = jnp.dot(hb, w1_ref[:, c * ff_chunk:(c + 1) * ff_chunk], preferred_element_type=F32)
        hid = jnp.square(jnp.maximum(hid, 0.0)).astype(BF16)
        acc = acc + jnp.dot(hid, w2_ref[c * ff_chunk:(c + 1) * ff_chunk, :], preferred_element_type=F32)
    o_ref[...] = x1 + mod_ref[5:6, :] * acc


def _mlp_call(layer, x, mod, mod_row, g2, att, ret, four, w_out, w1, w2, *, tm):
    b, t, d = x.shape
    d_ff = w1.shape[-1]
    tok = lambda w: pl.BlockSpec((None, tm, w), lambda i, j: (i, j, 0))
    once = dict(pipeline_mode=pl.Buffered(1))
    per_layer = lambda *shape, **kw: pl.BlockSpec((None,) + shape, lambda i, j: (layer,) + (0,) * len(shape), **kw)
    return pl.pallas_call(
        functools.partial(_mlp_kernel, ff_chunk=1024),
        grid=(b, t // tm),
        in_specs=[tok(d),
                  pl.BlockSpec((None, None, N_MOD, d), lambda i, j: (layer, mod_row(i), 0, 0)),
                  per_layer(1, d),
                  tok(ATT_W), tok(RET_W), tok(FOUR_W),
                  per_layer(d, d, **once),
                  per_layer(d, d_ff, **once),
                  per_layer(d_ff, d, **once)],
        out_specs=tok(d),
        out_shape=jax.ShapeDtypeStruct((b, t, d), F32),
        compiler_params=_cparams(2),
        name="out_projection_mlp",
    )(x, mod, g2, att, ret, four, w_out, w1, w2)


def _rope_tables(pos_groups, half):
    freqs = ROPE_BASE ** (-jnp.arange(half, dtype=F32) / half)
    cos, sin_next, sin_prev = [], [], []
    zeros = None
    for pos in pos_groups:
        ang = pos.astype(F32)[:, None] * freqs[None, :]
        c, s = jnp.cos(ang), jnp.sin(ang)
        zeros = jnp.zeros_like(s)
        cos += [c, c]
        sin_next += [-s, zeros]
        sin_prev += [zeros, s]
    reps = LANES // (2 * half * len(pos_groups))
    cat = lambda parts: jnp.tile(jnp.concatenate(parts, axis=1), (1, reps))
    return cat(cos), cat(sin_next), cat(sin_prev)


def _dft_tables(n):
    k = np.arange(n, dtype=np.int64)
    ang = 2.0 * np.pi * ((k[:, None] * k[None, :]) % n).astype(np.float64) / n
    return np.cos(ang), np.sin(ang)


def _block_diag(blocks):
    n = blocks.shape[-1]
    g = blocks.shape[-3]
    eye = jnp.eye(g, dtype=blocks.dtype)
    out = blocks[..., :, :, None, :] * eye[:, None, :, None]
    return out.reshape(blocks.shape[:-3] + (g * n, g * n))


def kernel(x, c, ctx, c_ctx, w_mod, b_mod, norm1_g, norm2_g, w_in, w_out, q_norm_g, k_norm_g, attn_sink,
           ret_decay_logit, ret_gn_g, fourier_w, w_ff1, w_ff2):
    b, s, d = x.shape
    lc = ctx.shape[1]
    depth = w_mod.shape[0]
    fdim = fourier_w.shape[-1]

    n_rows = -(-(b + 1) // 16) * 16
    cvec = jnp.zeros((n_rows, d), F32).at[:b].set(c).at[b].set(c_ctx)
    mod = _mod_call(cvec, w_mod, b_mod).reshape(depth, n_rows, N_MOD, d)

    pos = jnp.arange(s)
    tabs_lat = _rope_tables([pos // GRID_W, pos % GRID_W], HEAD_DIM // 4) + _rope_tables([pos], HEAD_DIM // 2)
    tabs_ctx = tuple(jnp.zeros((lc, LANES), F32) for _ in range(6))
    head_id = np.arange(256) // HEAD_DIM
    hsum = jnp.asarray(head_id[:, None] == head_id[None, :], BF16)
    havg = jnp.asarray((head_id[:, None] == head_id[None, :]) / HEAD_DIM, BF16)
    c_ch, s_ch = _dft_tables(fdim)
    eye_g = np.eye(FOURIER_GROUPS)
    c_bd = jnp.asarray(np.kron(eye_g, c_ch), F32)
    s_bd = jnp.asarray(np.kron(eye_g, s_ch), F32)

    def pos_tables(n):
        cn, sn = _dft_tables(n)
        norm = 1.0 / np.sqrt(float(n) * fdim)
        rows = n // 2 + 16
        blk = min(256, n // 2)
        r = np.arange(blk)
        flip = ((r[:, None] + r[None, :]) == blk).astype(np.float32)
        return (jnp.asarray(cn[:rows] * norm, F32).astype(BF16), jnp.asarray(-sn[:rows] * norm, F32).astype(BF16),
                jnp.asarray(flip, BF16))

    ftab_s = pos_tables(s)
    ftab_c = pos_tables(lc)
    win_bias = _window_bias(2 * BLOCK)
    logit_bound = (HEAD_DIM * Q_SCALE * 1.02) * jnp.max(jnp.abs(q_norm_g), axis=-1) * jnp.max(jnp.abs(k_norm_g), axis=-1)
    spread = logit_bound + jnp.maximum(logit_bound, jnp.max(attn_sink, axis=-1) * LOG2E)
    attn_aux = jnp.stack([logit_bound, (spread < 100.0).astype(F32)], axis=-1).astype(F32)

    wc, ws = _fourier_prep_call(_block_diag(fourier_w.astype(F32)), c_bd, s_bd)
    wcs = jnp.concatenate([wc, ws], axis=-1).astype(BF16)

    w_in_b = w_in.astype(BF16)
    w_out_b = w_out.astype(BF16)
    w1_b = w_ff1.astype(BF16)
    w2_b = w_ff2.astype(BF16)
    g1 = norm1_g.reshape(depth, 1, d)
    g2 = norm2_g.reshape(depth, 1, d)
    gq = jnp.tile(q_norm_g, (1, 256 // HEAD_DIM)).reshape(depth, 1, 256)
    gk = jnp.tile(k_norm_g, (1, LANES // HEAD_DIM)).reshape(depth, 1, LANES)
    log_gamma = -jax.nn.softplus(-ret_decay_logit.astype(F32))
    log_gamma_lanes = jnp.repeat(log_gamma, HEAD_DIM, axis=-1)

    lat_row = lambda i: i
    ctx_row = lambda i: b

    h = ctx
    for l in range(depth):
        need_ctx = l < depth - 1
        common = (g1, w_in_b, gq, gk, hsum, wcs)
        q, kz, vz, rq, rk, rv, gate, u = _inproj_call(l, x, mod, lat_row, *common, tabs_lat, rope=True, tm=512)
        cq, ckz, cvz, crq, crk, crv, cgate, cu = _inproj_call(l, h, mod, ctx_row, *common, tabs_ctx, rope=False, tm=lc)

        att = _attn_call(attn_sink[l], attn_aux[l], win_bias, q, kz, vz, ckz, cvz, band=True)
        ret, ret_c = _ret_call(log_gamma[l], log_gamma_lanes[l], ret_gn_g[l].reshape(1, RET_W), havg,
                               rq, rk, rv, gate, crq, crk, crv, cgate)
        four = _fourier_call(*ftab_s, u)
        x = _mlp_call(l, x, mod, lat_row, g2, att, ret, four, w_out_b, w1_b, w2_b, tm=512)
        if need_ctx:
            att_c = _attn_call(attn_sink[l], attn_aux[l], win_bias, cq, ckz, cvz, ckz, cvz, band=False)
            four_c = _fourier_call(*ftab_c, cu)
            h = _mlp_call(l, h, mod, ctx_row, g2, att_c, ret_c, four_c, w_out_b, w1_b, w2_b, tm=lc)
    return x
```

```python
import functools

import numpy as np
import jax
import jax.numpy as jnp
from jax import lax
from jax.experimental import pallas as pl
from jax.experimental.pallas import tpu as pltpu

F32 = jnp.float32
BF16 = jnp.bfloat16

HEAD_DIM = 64
LANES = 128
GRID_W = 64
N_MOD = 6
ATT_Q_HEADS = 8
ATT_KV_HEADS = 2
WINDOW = 128
BLOCK = 128
RET_HEADS = 4
RET_CHUNK = 128
FOURIER_GROUPS = 4
ROPE_BASE = 10000.0
EPS = 1e-6
NEG_INF = -1e30
VMEM_LIMIT_BYTES = 56 * 1024 * 1024

ATT_W = ATT_Q_HEADS * HEAD_DIM
KV_W = ATT_KV_HEADS * HEAD_DIM
RET_W = RET_HEADS * HEAD_DIM
FOUR_W = 256
O_Q, O_K, O_V = 0, ATT_W, ATT_W + KV_W
O_RQ = ATT_W + 2 * KV_W
O_RK, O_RV, O_GF, O_GB = O_RQ + RET_W, O_RQ + 2 * RET_W, O_RQ + 3 * RET_W, O_RQ + 4 * RET_W
O_FU = O_RQ + 5 * RET_W
PROJ_W = O_FU + FOUR_W
LOG2E = 1.4426950408889634
Q_SCALE = HEAD_DIM ** -0.5 * LOG2E


def _cparams(n_axes, flags=None):
    return pltpu.CompilerParams(dimension_semantics=("arbitrary",) * n_axes,
                                vmem_limit_bytes=VMEM_LIMIT_BYTES, flags=flags)


def _silu(v):
    return v / (1.0 + jnp.exp(-v))


def _mod_kernel(c_ref, w_ref, b_ref, o_ref):
    s = _silu(c_ref[...]).astype(BF16)
    o_ref[...] = jnp.dot(s, w_ref[...].astype(BF16), preferred_element_type=F32) + b_ref[...]


def _mod_call(cvec, w_mod, b_mod):
    depth, d, n = w_mod.shape
    r = cvec.shape[0]
    tn = 1536
    return pl.pallas_call(
        _mod_kernel,
        grid=(depth, n // tn),
        in_specs=[pl.BlockSpec((r, d), lambda l, j: (0, 0)),
                  pl.BlockSpec((None, d, tn), lambda l, j: (l, 0, j)),
                  pl.BlockSpec((None, 1, tn), lambda l, j: (l, 0, j))],
        out_specs=pl.BlockSpec((None, r, tn), lambda l, j: (l, 0, j)),
        out_shape=jax.ShapeDtypeStruct((depth, r, n), F32),
        compiler_params=_cparams(2),
        name="mod_vectors",
    )(cvec, w_mod, b_mod.reshape(depth, 1, n))


def _fourier_prep_kernel(w_ref, c_ref, s_ref, wc_ref, ws_ref):
    w = w_ref[...]
    wc_ref[...] = jnp.dot(c_ref[...], w, preferred_element_type=F32, precision=lax.Precision.HIGHEST)
    ws_ref[...] = jnp.dot(s_ref[...], w, preferred_element_type=F32, precision=lax.Precision.HIGHEST)


def _fourier_prep_call(w_bd, c_bd, s_bd):
    depth, n, _ = w_bd.shape
    spec_w = pl.BlockSpec((None, n, n), lambda l: (l, 0, 0))
    spec_t = pl.BlockSpec((n, n), lambda l: (0, 0))
    return pl.pallas_call(
        _fourier_prep_kernel,
        grid=(depth,),
        in_specs=[spec_w, spec_t, spec_t],
        out_specs=[spec_w, spec_w],
        out_shape=[jax.ShapeDtypeStruct((depth, n, n), F32)] * 2,
        compiler_params=_cparams(1),
        name="fourier_prep",
    )(w_bd, c_bd, s_bd)


def _rope(v, cos, sin_next, sin_prev, shift):
    return (v * cos + pltpu.roll(v, LANES - shift, 1) * sin_next + pltpu.roll(v, shift, 1) * sin_prev)


def _even_odd_variants(v, ones_lane=False):
    lane = lax.broadcasted_iota(jnp.int32, v.shape, 1)
    lo = lane < HEAD_DIM
    vr = pltpu.roll(v, HEAD_DIM, 1)
    pad_lo = jnp.where(lane == 0, 1.0, 0.0) if ones_lane else jnp.zeros_like(v)
    pad_hi = jnp.where(lane == HEAD_DIM, 1.0, 0.0) if ones_lane else jnp.zeros_like(v)
    return (jnp.where(lo, v, pad_hi), jnp.where(lo, pad_lo, vr), jnp.where(lo, vr, pad_hi), jnp.where(lo, pad_lo, v))


def _inproj_kernel(x_ref, mod_ref, g_ref, w_ref, gq_ref, gk_ref, hsum_ref, wcs_ref,
                   ca_ref, san_ref, sap_ref, ct_ref, stn_ref, stp_ref,
                   q_ref, kz_ref, vz_ref, rq_ref, rk_ref, rv_ref, gate_ref, u_ref, *, rope):
    hsum = hsum_ref[...]
    inv_d = 1.0 / HEAD_DIM

    def head_norm(v, gain):
        ssq = jnp.dot((v * v).astype(BF16), hsum, preferred_element_type=F32)
        return v * lax.rsqrt(ssq * inv_d + EPS) * gain

    x = x_ref[...]
    ms = jnp.mean(x * x, axis=-1, keepdims=True)
    a = ((x * lax.rsqrt(ms + EPS)) * g_ref[...]) * (1.0 + mod_ref[1:2, :]) + mod_ref[0:1, :]
    proj = jnp.dot(a.astype(BF16), w_ref[...], preferred_element_type=F32)
    if rope:
        att_tab = (ca_ref[...], san_ref[...], sap_ref[...], HEAD_DIM // 4)
        ret_tab = (ct_ref[...], stn_ref[...], stp_ref[...], HEAD_DIM // 2)

    for j in range(ATT_W // 256):
        qn = head_norm(proj[:, O_Q + j * 256:O_Q + (j + 1) * 256], gq_ref[...])
        for c in range(2):
            col = qn[:, c * LANES:(c + 1) * LANES]
            if rope:
                col = _rope(col, *att_tab)
            lo = j * 256 + c * LANES
            q_ref[:, lo:lo + LANES] = (col * Q_SCALE).astype(BF16)

    k = proj[:, O_K:O_V]
    ssq = jnp.dot((k * k).astype(BF16), hsum[0:LANES, 0:LANES], preferred_element_type=F32)
    k = k * lax.rsqrt(ssq * inv_d + EPS) * gk_ref[...]
    if rope:
        k = _rope(k, *att_tab)
    for i, var in enumerate(_even_odd_variants(k)):
        kz_ref[:, i * LANES:(i + 1) * LANES] = var.astype(BF16)
    for i, var in enumerate(_even_odd_variants(proj[:, O_V:O_RQ], ones_lane=True)):
        vz_ref[:, i * LANES:(i + 1) * LANES] = var.astype(BF16)

    for c in range(RET_W // LANES):
        qc = proj[:, O_RQ + c * LANES:O_RQ + (c + 1) * LANES]
        kc = proj[:, O_RK + c * LANES:O_RK + (c + 1) * LANES] * HEAD_DIM ** -0.5
        if rope:
            qc = _rope(qc, *ret_tab)
            kc = _rope(kc, *ret_tab)
        rq_ref[:, c * LANES:(c + 1) * LANES] = qc.astype(BF16)
        rk_ref[:, c * LANES:(c + 1) * LANES] = kc.astype(BF16)
    rv_ref[...] = proj[:, O_RV:O_GF].astype(BF16)
    gate_ref[...] = proj[:, O_GF:O_FU].astype(BF16)

    fu = proj[:, O_FU:PROJ_W].astype(BF16)
    u_ref[...] = jnp.dot(fu, wcs_ref[...], preferred_element_type=F32).astype(BF16)


def _inproj_call(layer, x, mod, mod_row, g1, w_in, gq, gk, hsum, wcs, tabs, *, rope, tm):
    b, t, d = x.shape
    per_layer = lambda *shape: pl.BlockSpec((None,) + shape, lambda j, i: (layer,) + (0,) * len(shape))
    tab = pl.BlockSpec((tm, LANES), lambda j, i: (j, 0))
    tok = lambda w: pl.BlockSpec((None, tm, w), lambda j, i: (i, j, 0))
    out_w = (ATT_W, 4 * LANES, 4 * LANES, RET_W, RET_W, RET_W, 2 * RET_W, 2 * FOUR_W)
    return pl.pallas_call(
        functools.partial(_inproj_kernel, rope=rope),
        grid=(t // tm, b),
        in_specs=[tok(d),
                  pl.BlockSpec((None, None, N_MOD, d), lambda j, i: (layer, mod_row(i), 0, 0)),
                  per_layer(1, d),
                  per_layer(d, PROJ_W),
                  per_layer(1, 256),
                  per_layer(1, LANES),
                  pl.BlockSpec((256, 256), lambda j, i: (0, 0)),
                  per_layer(FOUR_W, 2 * FOUR_W),
                  tab, tab, tab, tab, tab, tab],
        out_specs=[tok(w) for w in out_w],
        out_shape=[jax.ShapeDtypeStruct((b, t, w), BF16) for w in out_w],
        compiler_params=_cparams(2),
        name="in_projection",
    )(x, mod, g1, w_in, gq, gk, hsum, wcs, *tabs)


def _attn_kernel(sink_ref, aux_ref, bias_ref, q_ref, kz_ref, vz_ref, kcz_ref, vcz_ref, o_ref, *, band, t, tq, tiles):
    contract_last = (((1,), (1,)), ((), ()))
    span = tq + 2 * WINDOW
    starts, biases = [], []
    if band:
        last = t // tq - 1
        for sub in range(tiles):
            n = pl.program_id(1) * tiles + sub
            starts.append(pl.multiple_of(jnp.clip(n * tq - WINDOW, 0, t - span), BLOCK))
            biases.append(bias_ref[jnp.where(n == 0, 0, jnp.where(n == last, 2, 1))])
    upper = lax.broadcasted_iota(jnp.int32, (2 * tq, 1), 0) < tq
    lane_lo = lax.broadcasted_iota(jnp.int32, (2 * tq, LANES), 1) < HEAD_DIM
    units = [(sub, h) for sub in range(tiles) for h in range(ATT_KV_HEADS)]

    def run(use_bound):
        scores = {}
        for sub, h in units:
            rows = slice(sub * tq, (sub + 1) * tq)
            qq = jnp.concatenate([q_ref[rows, (2 * h) * LANES:(2 * h + 1) * LANES],
                                  q_ref[rows, (2 * h + 1) * LANES:(2 * h + 2) * LANES]], axis=0)
            for par in range(2):
                col = (2 * h + par) * LANES
                s_ctx = lax.dot_general(qq, kcz_ref[:, col:col + LANES], contract_last, preferred_element_type=F32)
                s_band = None
                if band:
                    s_band = lax.dot_general(qq, kz_ref[pl.ds(starts[sub], span), col:col + LANES], contract_last,
                                             preferred_element_type=F32)
                scores[sub, h, par] = (s_ctx, s_band)
        for sub, h in units:
            rows = slice(sub * tq, (sub + 1) * tq)
            scaled = []
            for par in range(2):
                col = (2 * h + par) * LANES
                s_ctx, s_band = scores[sub, h, par]
                sink = jnp.where(upper, sink_ref[4 * h + par], sink_ref[4 * h + 2 + par]) * LOG2E
                if use_bound:
                    m = jnp.maximum(aux_ref[0], sink)
                    if band:
                        s_band = s_band + (biases[sub] - m)
                else:
                    m = jnp.maximum(jnp.max(s_ctx, axis=-1, keepdims=True), sink)
                    if band:
                        s_band = s_band + biases[sub]
                        m = jnp.maximum(m, jnp.max(s_band, axis=-1, keepdims=True))
                        s_band = s_band - m
                acc = jnp.dot(jnp.exp2(s_ctx - m).astype(BF16), vcz_ref[:, col:col + LANES],
                              preferred_element_type=F32)
                if band:
                    acc = acc + jnp.dot(jnp.exp2(s_band).astype(BF16),
                                        vz_ref[pl.ds(starts[sub], span), col:col + LANES], preferred_element_type=F32)
                ones = HEAD_DIM if par == 0 else 0
                den = acc[:, ones:ones + 1] + jnp.exp2(sink - m)
                scaled.append(acc * (1.0 / den))
            out = jnp.where(lane_lo, scaled[0], scaled[1])
            o_ref[rows, (2 * h) * LANES:(2 * h + 1) * LANES] = out[0:tq].astype(BF16)
            o_ref[rows, (2 * h + 1) * LANES:(2 * h + 2) * LANES] = out[tq:2 * tq].astype(BF16)

    pl.when(aux_ref[1] > 0.5)(lambda: run(True))
    pl.when(aux_ref[1] <= 0.5)(lambda: run(False))


def _attn_call(sink, aux, bias, q, kz, vz, kcz, vcz, *, band):
    b, t, _ = q.shape
    lc = kcz.shape[1]
    tq = bias.shape[1] // 2
    tiles = 2 if t % (2 * tq) == 0 else 1
    full = lambda rows: pl.BlockSpec((None, rows, 4 * LANES), lambda i, j: (i, 0, 0))
    blk = pl.BlockSpec((None, tiles * tq, ATT_W), lambda i, j: (i, j, 0))
    return pl.pallas_call(
        functools.partial(_attn_kernel, band=band, t=t, tq=tq, tiles=tiles),
        grid=(b, t // (tiles * tq)),
        in_specs=[pl.BlockSpec(memory_space=pltpu.SMEM), pl.BlockSpec(memory_space=pltpu.SMEM),
                  pl.BlockSpec(bias.shape, lambda i, j: (0, 0, 0)),
                  blk, full(kz.shape[1]), full(vz.shape[1]), full(lc), full(lc)],
        out_specs=blk,
        out_shape=jax.ShapeDtypeStruct((b, t, ATT_W), BF16),
        compiler_params=_cparams(2),
        name="window_attention" if band else "context_attention",
    )(sink, aux, bias, q, kz, vz, kcz, vcz)


def _window_bias(tq):
    span = tq + 2 * WINDOW
    r = np.arange(2 * tq)[:, None] % tq
    c = np.arange(span)[None, :]
    cases = [np.where(np.abs(c - r - off) <= WINDOW, 0.0, NEG_INF) for off in (0, WINDOW, 2 * WINDOW)]
    return jnp.asarray(np.stack(cases), F32)


def _ret_kernel(lg_ref, lgl_ref, gn_ref, havg_ref,
                rq_ref, rk_ref, rv_ref, gate_ref, cq_ref, ck_ref, cv_ref, cgate_ref,
                o_ref, oc_ref,
                dec_ref, st_ref, of_ref, ob_ref, ocf_ref, ocb_ref, *, t, lc):
    c = RET_CHUNK

    @pl.when(pl.program_id(0) == 0)
    def _():
        ri = lax.broadcasted_iota(jnp.int32, (c, c), 0)
        ci = lax.broadcasted_iota(jnp.int32, (c, c), 1)
        diff = (ri - ci).astype(F32)
        for h in range(RET_HEADS):
            dec_ref[0, :, h * c:(h + 1) * c] = jnp.where(ri >= ci, jnp.exp(jnp.maximum(diff, 0.0) * lg_ref[0, h]), 0.0)
            dec_ref[1, :, h * c:(h + 1) * c] = jnp.where(ci >= ri, jnp.exp(jnp.maximum(-diff, 0.0) * lg_ref[1, h]), 0.0)

    idx = lax.broadcasted_iota(jnp.int32, (c, RET_W), 0).astype(F32)
    lgf = lgl_ref[0:1, :]
    lgb = lgl_ref[1:2, :]
    xi = (jnp.exp((idx + 1.0) * lgf), jnp.exp((c - idx) * lgb))
    zeta = (jnp.exp((c - 1.0 - idx) * lgf), jnp.exp(idx * lgb))
    cdec = (jnp.exp(c * lgf), jnp.exp(c * lgb))
    lane_head = lax.broadcasted_iota(jnp.int32, (1, RET_W), 1) // HEAD_DIM
    head_mask = [lane_head == h for h in range(RET_HEADS)]
    blockdiag = (lax.broadcasted_iota(jnp.int32, (RET_W, RET_W), 0) // HEAD_DIM
                 == lax.broadcasted_iota(jnp.int32, (RET_W, RET_W), 1) // HEAD_DIM)
    contract_last = (((1,), (1,)), ((), ()))
    contract_first = (((0,), (0,)), ((), ()))

    def by_head(v):
        zero = jnp.zeros_like(v)
        return jnp.concatenate([jnp.where(head_mask[h], v, zero) for h in range(RET_HEADS)], axis=0)

    st_ref[...] = jnp.zeros_like(st_ref)

    def run(n_chunks, q_ref_, k_ref_, v_ref_, out_refs):
        group = min(8, n_chunks)

        def body(i, carry):
            work = []
            for u in range(group):
                for direction in (0, 1):
                    idx = i * group + u
                    r0 = pl.multiple_of((idx if direction == 0 else n_chunks - 1 - idx) * c, c)
                    q, k, v = q_ref_[pl.ds(r0, c), :], k_ref_[pl.ds(r0, c), :], v_ref_[pl.ds(r0, c), :]
                    s = lax.dot_general(q, by_head(k), contract_last, preferred_element_type=F32)
                    kd = (k.astype(F32) * zeta[direction]).astype(BF16)
                    kv = lax.dot_general(kd, v, contract_first, preferred_element_type=F32)
                    work.append((direction, r0, q, v, s, kv))
            cross = []
            for direction, r0, q, v, s, kv in work:
                st = st_ref[direction]
                cross.append(jnp.dot(q, st.astype(BF16), preferred_element_type=F32) * xi[direction])
                st_ref[direction] = st * cdec[direction] + jnp.where(blockdiag, kv, 0.0)
            for (direction, r0, q, v, s, kv), cr in zip(work, cross):
                p = (s * dec_ref[direction]).astype(BF16)
                out_refs[direction][pl.ds(r0, c), :] = jnp.dot(p, by_head(v), preferred_element_type=F32) + cr
            return carry

        lax.fori_loop(0, n_chunks // group, body, 0)

    run(lc // c, cq_ref, ck_ref, cv_ref, (ocf_ref, ocb_ref))
    run(t // c, rq_ref, rk_ref, rv_ref, (of_ref, ob_ref))

    havg = havg_ref[...]
    gn_g = gn_ref[...]

    def standardize(o):
        mu = jnp.dot(o.astype(BF16), havg, preferred_element_type=F32)
        d = o - mu
        var = jnp.dot((d * d).astype(BF16), havg, preferred_element_type=F32)
        return d * lax.rsqrt(var + EPS)

    def finish(of, ob, gates):
        gf = gates[:, 0:RET_W].astype(F32)
        gb = gates[:, RET_W:2 * RET_W].astype(F32)
        return ((standardize(of) * _silu(gf) + standardize(ob) * _silu(gb)) * gn_g).astype(BF16)

    o_ref[...] = finish(of_ref[...], ob_ref[...], gate_ref[...])
    oc_ref[...] = finish(ocf_ref[...], ocb_ref[...], cgate_ref[...])


def _ret_call(lg, lgl, gn_g, havg, rq, rk, rv, gate, cq, ck, cv, cgate):
    b, t, _ = rq.shape
    lc = cq.shape[1]
    row = lambda i: (0, 0)
    seq = lambda rows, w: pl.BlockSpec((None, rows, w), lambda i: (i, 0, 0))
    return pl.pallas_call(
        functools.partial(_ret_kernel, t=t, lc=lc),
        grid=(b,),
        in_specs=[pl.BlockSpec(memory_space=pltpu.SMEM),
                  pl.BlockSpec((2, RET_W), row), pl.BlockSpec((1, RET_W), row), pl.BlockSpec((RET_W, RET_W), row),
                  seq(t, RET_W), seq(t, RET_W), seq(t, RET_W), seq(t, 2 * RET_W),
                  seq(lc, RET_W), seq(lc, RET_W), seq(lc, RET_W), seq(lc, 2 * RET_W)],
        out_specs=[seq(t, RET_W), seq(lc, RET_W)],
        out_shape=[jax.ShapeDtypeStruct((b, t, RET_W), BF16), jax.ShapeDtypeStruct((b, lc, RET_W), BF16)],
        scratch_shapes=[pltpu.VMEM((2, RET_CHUNK, RET_HEADS * RET_CHUNK), F32),
                        pltpu.VMEM((2, RET_W, RET_W), F32),
                        pltpu.VMEM((t, RET_W), F32), pltpu.VMEM((t, RET_W), F32),
                        pltpu.VMEM((lc, RET_W), F32), pltpu.VMEM((lc, RET_W), F32)],
        compiler_params=_cparams(1),
        name="retention",
    )(lg, lgl, gn_g, havg, rq, rk, rv, gate, cq, ck, cv, cgate)


def _fourier_kernel(c_ref, s_ref, flip_ref, u_ref, o_ref, *, t, blk, norm):
    half = t // 2
    n_blk = half // blk
    row0_wide = lax.broadcasted_iota(jnp.int32, (blk, 2 * FOUR_W), 0) == 0
    folded_c, folded_s = [], []
    for i in range(n_blk):
        rev = jnp.dot(flip_ref[...], u_ref[t - (i + 1) * blk:t - i * blk, :], preferred_element_type=F32)
        if i > 0:
            rev = jnp.where(row0_wide, u_ref[t - i * blk:t - i * blk + 1, :].astype(F32), rev)
        low = u_ref[i * blk:(i + 1) * blk, :].astype(F32)
        folded_c.append((low[:, 0:FOUR_W] + rev[:, 0:FOUR_W]).astype(BF16))
        folded_s.append((low[:, FOUR_W:] - rev[:, FOUR_W:]).astype(BF16))
    rows = c_ref.shape[0]
    sign = norm * (1.0 - 2.0 * (lax.broadcasted_iota(jnp.int32, (rows, 1), 0) % 2).astype(F32))
    a = (jnp.dot(c_ref[...], jnp.concatenate(folded_c, axis=0), preferred_element_type=F32)
         + sign * u_ref[half:half + 1, 0:FOUR_W].astype(F32))
    bs = jnp.dot(s_ref[...], jnp.concatenate(folded_s, axis=0), preferred_element_type=F32)
    o_ref[0:half, :] = (a[0:half] + bs[0:half]).astype(BF16)
    mid = (a[half:half + 1] + bs[half:half + 1]).astype(BF16)
    mirror = (a[0:half] - bs[0:half]).astype(BF16)
    row0 = lax.broadcasted_iota(jnp.int32, (blk, FOUR_W), 0) == 0
    for i in range(n_blk):
        src = mirror[(n_blk - 1 - i) * blk:(n_blk - i) * blk]
        rev = jnp.dot(flip_ref[...], src, preferred_element_type=F32).astype(BF16)
        first = mid if i == 0 else mirror[(n_blk - i) * blk:(n_blk - i) * blk + 1]
        o_ref[half + i * blk:half + (i + 1) * blk, :] = jnp.where(row0, first, rev)


def _fourier_call(cmat, smat_neg, flip, u):
    b, t, _ = u.shape
    rows = cmat.shape[0]
    blk = flip.shape[0]
    const = lambda *shape: pl.BlockSpec(shape, lambda i: (0,) * len(shape))
    return pl.pallas_call(
        functools.partial(_fourier_kernel, t=t, blk=blk, norm=float(t * (FOUR_W // FOURIER_GROUPS)) ** -0.5),
        grid=(b,),
        in_specs=[const(rows, t // 2), const(rows, t // 2), const(blk, blk),
                  pl.BlockSpec((None, t, 2 * FOUR_W), lambda i: (i, 0, 0))],
        out_specs=pl.BlockSpec((None, t, FOUR_W), lambda i: (i, 0, 0)),
        out_shape=jax.ShapeDtypeStruct((b, t, FOUR_W), BF16),
        compiler_params=_cparams(1),
        name="fourier_mix",
    )(cmat, smat_neg, flip, u)


def _mlp_kernel(x_ref, mod_ref, g_ref, att_ref, ret_ref, four_ref, wo_ref, w1_ref, w2_ref, o_ref, *, ff_chunk):
    m = (jnp.dot(att_ref[...], wo_ref[0:ATT_W, :], preferred_element_type=F32)
         + jnp.dot(ret_ref[...], wo_ref[ATT_W:ATT_W + RET_W, :], preferred_element_type=F32)
         + jnp.dot(four_ref[...], wo_ref[ATT_W + RET_W:, :], preferred_element_type=F32))
    x1 = x_ref[...] + mod_ref[2:3, :] * m
    ms = jnp.mean(x1 * x1, axis=-1, keepdims=True)
    hn = ((x1 * lax.rsqrt(ms + EPS)) * g_ref[...]) * (1.0 + mod_ref[4:5, :]) + mod_ref[3:4, :]
    hb = hn.astype(BF16)
    d_ff = w1_ref.shape[1]
    acc = jnp.zeros(x1.shape, F32)
    for c in range(d_ff // ff_chunk):
        hid = jnp.dot(hb, w1_ref[:, c * ff_chunk:(c + 1) * ff_chunk], preferred_element_type=F32)
        hid = jnp.square(jnp.maximum(hid, 0.0)).astype(BF16)
        acc = acc + jnp.dot(hid, w2_ref[c * ff_chunk:(c + 1) * ff_chunk, :], preferred_element_type=F32)
    o_ref[...] = x1 + mod_ref[5:6, :] * acc


def _mlp_call(layer, x, mod, mod_row, g2, att, ret, four, w_out, w1, w2, *, tm):
    b, t, d = x.shape
    d_ff = w1.shape[-1]
    tok = lambda w: pl.BlockSpec((None, tm, w), lambda i, j: (i, j, 0))
    once = dict(pipeline_mode=pl.Buffered(1))
    per_layer = lambda *shape, **kw: pl.BlockSpec((None,) + shape, lambda i, j: (layer,) + (0,) * len(shape), **kw)
    return pl.pallas_call(
        functools.partial(_mlp_kernel, ff_chunk=1024),
        grid=(b, t // tm),
        in_specs=[tok(d),
                  pl.BlockSpec((None, None, N_MOD, d), lambda i, j: (layer, mod_row(i), 0, 0)),
                  per_layer(1, d),
                  tok(ATT_W), tok(RET_W), tok(FOUR_W),
                  per_layer(d, d, **once),
                  per_layer(d, d_ff, **once),
                  per_layer(d_ff, d, **once)],
        out_specs=tok(d),
        out_shape=jax.ShapeDtypeStruct((b, t, d), F32),
        compiler_params=_cparams(2),
        name="out_projection_mlp",
    )(x, mod, g2, att, ret, four, w_out, w1, w2)


def _rope_tables(pos_groups, half):
    freqs = ROPE_BASE ** (-jnp.arange(half, dtype=F32) / half)
    cos, sin_next, sin_prev = [], [], []
    zeros = None
    for pos in pos_groups:
        ang = pos.astype(F32)[:, None] * freqs[None, :]
        c, s = jnp.cos(ang), jnp.sin(ang)
        zeros = jnp.zeros_like(s)
        cos += [c, c]
        sin_next += [-s, zeros]
        sin_prev += [zeros, s]
    reps = LANES // (2 * half * len(pos_groups))
    cat = lambda parts: jnp.tile(jnp.concatenate(parts, axis=1), (1, reps))
    return cat(cos), cat(sin_next), cat(sin_prev)


def _dft_tables(n):
    k = np.arange(n, dtype=np.int64)
    ang = 2.0 * np.pi * ((k[:, None] * k[None, :]) % n).astype(np.float64) / n
    return np.cos(ang), np.sin(ang)


def _block_diag(blocks):
    n = blocks.shape[-1]
    g = blocks.shape[-3]
    eye = jnp.eye(g, dtype=blocks.dtype)
    out = blocks[..., :, :, None, :] * eye[:, None, :, None]
    return out.reshape(blocks.shape[:-3] + (g * n, g * n))


def kernel(x, c, ctx, c_ctx, w_mod, b_mod, norm1_g, norm2_g, w_in, w_out, q_norm_g, k_norm_g, attn_sink,
           ret_decay_logit, ret_gn_g, fourier_w, w_ff1, w_ff2):
    b, s, d = x.shape
    lc = ctx.shape[1]
    depth = w_mod.shape[0]
    fdim = fourier_w.shape[-1]

    n_rows = -(-(b + 1) // 16) * 16
    cvec = jnp.zeros((n_rows, d), F32).at[:b].set(c).at[b].set(c_ctx)
    mod = _mod_call(cvec, w_mod, b_mod).reshape(depth, n_rows, N_MOD, d)

    pos = jnp.arange(s)
    tabs_lat = _rope_tables([pos // GRID_W, pos % GRID_W], HEAD_DIM // 4) + _rope_tables([pos], HEAD_DIM // 2)
    pair = 2 if b % 2 == 0 else 1
    paired = lambda v: v.reshape(b // pair, pair * lc, v.shape[-1])
    unpaired = lambda v: v.reshape(b, lc, v.shape[-1])
    tabs_ctx = tuple(jnp.zeros((pair * lc, LANES), F32) for _ in range(6))
    head_id = np.arange(256) // HEAD_DIM
    hsum = jnp.asarray(head_id[:, None] == head_id[None, :], BF16)
    havg = jnp.asarray((head_id[:, None] == head_id[None, :]) / HEAD_DIM, BF16)
    c_ch, s_ch = _dft_tables(fdim)
    eye_g = np.eye(FOURIER_GROUPS)
    c_bd = jnp.asarray(np.kron(eye_g, c_ch), F32)
    s_bd = jnp.asarray(np.kron(eye_g, s_ch), F32)

    def pos_tables(n):
        cn, sn = _dft_tables(n)
        norm = 1.0 / np.sqrt(float(n) * fdim)
        rows = n // 2 + 16
        blk = min(256, n // 2)
        r = np.arange(blk)
        flip = ((r[:, None] + r[None, :]) == blk).astype(np.float32)
        cols = n // 2
        return (jnp.asarray(cn[:rows, :cols] * norm, F32).astype(BF16),
                jnp.asarray(-sn[:rows, :cols] * norm, F32).astype(BF16), jnp.asarray(flip, BF16))

    ftab_s = pos_tables(s)
    ftab_c = pos_tables(lc)
    win_bias = _window_bias(2 * BLOCK)
    logit_bound = (HEAD_DIM * Q_SCALE * 1.02) * jnp.max(jnp.abs(q_norm_g), axis=-1) * jnp.max(jnp.abs(k_norm_g), axis=-1)
    spread = logit_bound + jnp.maximum(logit_bound, jnp.max(attn_sink, axis=-1) * LOG2E)
    attn_aux = jnp.stack([logit_bound, (spread < 100.0).astype(F32)], axis=-1).astype(F32)

    wc, ws = _fourier_prep_call(_block_diag(fourier_w.astype(F32)), c_bd, s_bd)
    wcs = jnp.concatenate([wc, ws], axis=-1).astype(BF16)

    w_in_b = w_in.astype(BF16)
    w_out_b = w_out.astype(BF16)
    w1_b = w_ff1.astype(BF16)
    w2_b = w_ff2.astype(BF16)
    g1 = norm1_g.reshape(depth, 1, d)
    g2 = norm2_g.reshape(depth, 1, d)
    gq = jnp.tile(q_norm_g, (1, 256 // HEAD_DIM)).reshape(depth, 1, 256)
    gk = jnp.tile(k_norm_g, (1, LANES // HEAD_DIM)).reshape(depth, 1, LANES)
    log_gamma = -jax.nn.softplus(-ret_decay_logit.astype(F32))
    log_gamma_lanes = jnp.repeat(log_gamma, HEAD_DIM, axis=-1)

    lat_row = lambda i: i
    ctx_row = lambda i: b

    h = ctx
    for l in range(depth):
        need_ctx = l < depth - 1
        common = (g1, w_in_b, gq, gk, hsum, wcs)
        q, kz, vz, rq, rk, rv, gate, u = _inproj_call(l, x, mod, lat_row, *common, tabs_lat, rope=True, tm=512)
        ctx_out = _inproj_call(l, paired(h), mod, ctx_row, *common, tabs_ctx, rope=False, tm=pair * lc)
        cq, ckz, cvz, crq, crk, crv, cgate, cu = (unpaired(o) for o in ctx_out)

        att = _attn_call(attn_sink[l], attn_aux[l], win_bias, q, kz, vz, ckz, cvz, band=True)
        ret, ret_c = _ret_call(log_gamma[l], log_gamma_lanes[l], ret_gn_g[l].reshape(1, RET_W), havg,
                               rq, rk, rv, gate, crq, crk, crv, cgate)
        four = _fourier_call(*ftab_s, u)
        x = _mlp_call(l, x, mod, lat_row, g2, att, ret, four, w_out_b, w1_b, w2_b, tm=512)
        if need_ctx:
            att_c = _attn_call(attn_sink[l], attn_aux[l], win_bias, cq, ckz, cvz, ckz, cvz, band=False)
            four_c = _fourier_call(*ftab_c, cu)
            h = unpaired(_mlp_call(l, paired(h), mod, ctx_row, g2, paired(att_c), paired(ret_c), paired(four_c),
                                   w_out_b, w1_b, w2_b, tm=pair * lc))
    return x
```

```python
import functools

import numpy as np
import jax
import jax.numpy as jnp
from jax import lax
from jax.experimental import pallas as pl
from jax.experimental.pallas import tpu as pltpu

F32 = jnp.float32
BF16 = jnp.bfloat16

HEAD_DIM = 64
LANES = 128
GRID_W = 64
N_MOD = 6
ATT_Q_HEADS = 8
ATT_KV_HEADS = 2
WINDOW = 128
BLOCK = 128
RET_HEADS = 4
RET_CHUNK = 128
FOURIER_GROUPS = 4
ROPE_BASE = 10000.0
EPS = 1e-6
NEG_INF = -1e30
VMEM_LIMIT_BYTES = 56 * 1024 * 1024

ATT_W = ATT_Q_HEADS * HEAD_DIM
KV_W = ATT_KV_HEADS * HEAD_DIM
RET_W = RET_HEADS * HEAD_DIM
FOUR_W = 256
O_Q, O_K, O_V = 0, ATT_W, ATT_W + KV_W
O_RQ = ATT_W + 2 * KV_W
O_RK, O_RV, O_GF, O_GB = O_RQ + RET_W, O_RQ + 2 * RET_W, O_RQ + 3 * RET_W, O_RQ + 4 * RET_W
O_FU = O_RQ + 5 * RET_W
PROJ_W = O_FU + FOUR_W
LOG2E = 1.4426950408889634
Q_SCALE = HEAD_DIM ** -0.5 * LOG2E


def _cparams(n_axes, flags=None):
    return pltpu.CompilerParams(dimension_semantics=("arbitrary",) * n_axes,
                                vmem_limit_bytes=VMEM_LIMIT_BYTES, flags=flags)


def _silu(v):
    return v / (1.0 + jnp.exp(-v))


def _mod_kernel(c_ref, w_ref, b_ref, o_ref):
    s = _silu(c_ref[...]).astype(BF16)
    o_ref[...] = jnp.dot(s, w_ref[...].astype(BF16), preferred_element_type=F32) + b_ref[...]


def _mod_call(cvec, w_mod, b_mod):
    depth, d, n = w_mod.shape
    r = cvec.shape[0]
    tn = 1536
    return pl.pallas_call(
        _mod_kernel,
        grid=(depth, n // tn),
        in_specs=[pl.BlockSpec((r, d), lambda l, j: (0, 0)),
                  pl.BlockSpec((None, d, tn), lambda l, j: (l, 0, j)),
                  pl.BlockSpec((None, 1, tn), lambda l, j: (l, 0, j))],
        out_specs=pl.BlockSpec((None, r, tn), lambda l, j: (l, 0, j)),
        out_shape=jax.ShapeDtypeStruct((depth, r, n), F32),
        compiler_params=_cparams(2),
        name="mod_vectors",
    )(cvec, w_mod, b_mod.reshape(depth, 1, n))


def _fourier_prep_kernel(w_ref, c_ref, s_ref, wc_ref, ws_ref):
    w = w_ref[...]
    wc_ref[...] = jnp.dot(c_ref[...], w, preferred_element_type=F32, precision=lax.Precision.HIGHEST)
    ws_ref[...] = jnp.dot(s_ref[...], w, preferred_element_type=F32, precision=lax.Precision.HIGHEST)


def _fourier_prep_call(w_bd, c_bd, s_bd):
    depth, n, _ = w_bd.shape
    spec_w = pl.BlockSpec((None, n, n), lambda l: (l, 0, 0))
    spec_t = pl.BlockSpec((n, n), lambda l: (0, 0))
    return pl.pallas_call(
        _fourier_prep_kernel,
        grid=(depth,),
        in_specs=[spec_w, spec_t, spec_t],
        out_specs=[spec_w, spec_w],
        out_shape=[jax.ShapeDtypeStruct((depth, n, n), F32)] * 2,
        compiler_params=_cparams(1),
        name="fourier_prep",
    )(w_bd, c_bd, s_bd)


def _rope(v, cos, sin_next, sin_prev, shift):
    return (v * cos + pltpu.roll(v, LANES - shift, 1) * sin_next + pltpu.roll(v, shift, 1) * sin_prev)


def _even_odd_variants(v, ones_lane=False):
    lane = lax.broadcasted_iota(jnp.int32, v.shape, 1)
    lo = lane < HEAD_DIM
    vr = pltpu.roll(v, HEAD_DIM, 1)
    pad_lo = jnp.where(lane == 0, 1.0, 0.0) if ones_lane else jnp.zeros_like(v)
    pad_hi = jnp.where(lane == HEAD_DIM, 1.0, 0.0) if ones_lane else jnp.zeros_like(v)
    return (jnp.where(lo, v, pad_hi), jnp.where(lo, pad_lo, vr), jnp.where(lo, vr, pad_hi), jnp.where(lo, pad_lo, v))


def _inproj_kernel(x_ref, mod_ref, g_ref, w_ref, gq_ref, gk_ref, havg_ref, wcs_ref,
                   ca_ref, san_ref, sap_ref, ct_ref, stn_ref, stp_ref,
                   q_ref, kz_ref, vz_ref, rq_ref, rk_ref, rv_ref, gate_ref, u_ref, *, rope):
    havg = havg_ref[...]

    def head_norm(v, gain):
        mean_sq = jnp.dot((v * v).astype(BF16), havg, preferred_element_type=F32)
        return v * lax.rsqrt(mean_sq + EPS) * gain

    x = x_ref[...]
    ms = jnp.mean(x * x, axis=-1, keepdims=True)
    a = (x * lax.rsqrt(ms + EPS)) * (g_ref[...] * (1.0 + mod_ref[1:2, :])) + mod_ref[0:1, :]
    proj = jnp.dot(a.astype(BF16), w_ref[...], preferred_element_type=F32)
    if rope:
        att_tab = (ca_ref[...], san_ref[...], sap_ref[...], HEAD_DIM // 4)
        ret_tab = (ct_ref[...], stn_ref[...], stp_ref[...], HEAD_DIM // 2)

    for j in range(ATT_W // 256):
        qn = head_norm(proj[:, O_Q + j * 256:O_Q + (j + 1) * 256], gq_ref[...])
        for c in range(2):
            col = qn[:, c * LANES:(c + 1) * LANES]
            if rope:
                col = _rope(col, *att_tab)
            lo = j * 256 + c * LANES
            q_ref[:, lo:lo + LANES] = col.astype(BF16)

    k = proj[:, O_K:O_V]
    mean_sq = jnp.dot((k * k).astype(BF16), havg[0:LANES, 0:LANES], preferred_element_type=F32)
    k = k * lax.rsqrt(mean_sq + EPS) * gk_ref[...]
    if rope:
        k = _rope(k, *att_tab)
    for i, var in enumerate(_even_odd_variants(k)):
        kz_ref[:, i * LANES:(i + 1) * LANES] = var.astype(BF16)
    for i, var in enumerate(_even_odd_variants(proj[:, O_V:O_RQ], ones_lane=True)):
        vz_ref[:, i * LANES:(i + 1) * LANES] = var.astype(BF16)

    for c in range(RET_W // LANES):
        qc = proj[:, O_RQ + c * LANES:O_RQ + (c + 1) * LANES]
        kc = proj[:, O_RK + c * LANES:O_RK + (c + 1) * LANES] * HEAD_DIM ** -0.5
        if rope:
            qc = _rope(qc, *ret_tab)
            kc = _rope(kc, *ret_tab)
        rq_ref[:, c * LANES:(c + 1) * LANES] = qc.astype(BF16)
        rk_ref[:, c * LANES:(c + 1) * LANES] = kc.astype(BF16)
    rv_ref[...] = proj[:, O_RV:O_GF].astype(BF16)
    gate_ref[...] = proj[:, O_GF:O_FU].astype(BF16)

    fu = proj[:, O_FU:PROJ_W].astype(BF16)
    u_ref[...] = jnp.dot(fu, wcs_ref[...], preferred_element_type=F32).astype(BF16)


def _inproj_call(layer, x, mod, mod_row, g1, w_in, gq, gk, havg, wcs, tabs, *, rope, tm):
    b, t, d = x.shape
    per_layer = lambda *shape: pl.BlockSpec((None,) + shape, lambda j, i: (layer,) + (0,) * len(shape))
    tab = pl.BlockSpec((tm, LANES), lambda j, i: (j, 0))
    tok = lambda w: pl.BlockSpec((None, tm, w), lambda j, i: (i, j, 0))
    out_w = (ATT_W, 4 * LANES, 4 * LANES, RET_W, RET_W, RET_W, 2 * RET_W, 2 * FOUR_W)
    return pl.pallas_call(
        functools.partial(_inproj_kernel, rope=rope),
        grid=(t // tm, b),
        in_specs=[tok(d),
                  pl.BlockSpec((None, None, N_MOD, d), lambda j, i: (layer, mod_row(i), 0, 0)),
                  per_layer(1, d),
                  per_layer(d, PROJ_W),
                  per_layer(1, 256),
                  per_layer(1, LANES),
                  pl.BlockSpec((256, 256), lambda j, i: (0, 0)),
                  per_layer(FOUR_W, 2 * FOUR_W),
                  tab, tab, tab, tab, tab, tab],
        out_specs=[tok(w) for w in out_w],
        out_shape=[jax.ShapeDtypeStruct((b, t, w), BF16) for w in out_w],
        compiler_params=_cparams(2),
        name="in_projection",
    )(x, mod, g1, w_in, gq, gk, havg, wcs, *tabs)


def _attn_kernel(sink_ref, aux_ref, bias_ref, q_ref, kz_ref, vz_ref, kcz_ref, vcz_ref, o_ref, *, band, t, tq, tiles):
    contract_last = (((1,), (1,)), ((), ()))
    span = tq + 2 * WINDOW
    starts, biases = [], []
    if band:
        last = t // tq - 1
        for sub in range(tiles):
            n = pl.program_id(1) * tiles + sub
            starts.append(pl.multiple_of(jnp.clip(n * tq - WINDOW, 0, t - span), BLOCK))
            biases.append(bias_ref[jnp.where(n == 0, 0, jnp.where(n == last, 2, 1))])
    upper = lax.broadcasted_iota(jnp.int32, (2 * tq, 1), 0) < tq
    lane_lo = lax.broadcasted_iota(jnp.int32, (2 * tq, LANES), 1) < HEAD_DIM
    units = [(sub, h) for sub in range(tiles) for h in range(ATT_KV_HEADS)]

    def run(use_bound):
        shifted = [bias - aux_ref[0] for bias in biases] if use_bound else None
        scores = {}
        for sub, h in units:
            rows = slice(sub * tq, (sub + 1) * tq)
            qq = jnp.concatenate([q_ref[rows, (2 * h) * LANES:(2 * h + 1) * LANES],
                                  q_ref[rows, (2 * h + 1) * LANES:(2 * h + 2) * LANES]], axis=0)
            for par in range(2):
                col = (2 * h + par) * LANES
                s_ctx = lax.dot_general(qq, kcz_ref[:, col:col + LANES], contract_last, preferred_element_type=F32)
                s_band = None
                if band:
                    s_band = lax.dot_general(qq, kz_ref[pl.ds(starts[sub], span), col:col + LANES], contract_last,
                                             preferred_element_type=F32)
                scores[sub, h, par] = (s_ctx, s_band)
        for sub, h in units:
            rows = slice(sub * tq, (sub + 1) * tq)
            scaled = []
            for par in range(2):
                col = (2 * h + par) * LANES
                s_ctx, s_band = scores[sub, h, par]
                sink = jnp.where(upper, sink_ref[4 * h + par], sink_ref[4 * h + 2 + par]) * LOG2E
                if use_bound:
                    m = aux_ref[0]
                    if band:
                        s_band = s_band + shifted[sub]
                else:
                    m = jnp.maximum(jnp.max(s_ctx, axis=-1, keepdims=True), sink)
                    if band:
                        s_band = s_band + biases[sub]
                        m = jnp.maximum(m, jnp.max(s_band, axis=-1, keepdims=True))
                        s_band = s_band - m
                acc = jnp.dot(jnp.exp2(s_ctx - m).astype(BF16), vcz_ref[:, col:col + LANES],
                              preferred_element_type=F32)
                if band:
                    acc = acc + jnp.dot(jnp.exp2(s_band).astype(BF16),
                                        vz_ref[pl.ds(starts[sub], span), col:col + LANES], preferred_element_type=F32)
                ones = HEAD_DIM if par == 0 else 0
                den = acc[:, ones:ones + 1] + jnp.exp2(sink - m)
                scaled.append(acc * (1.0 / den))
            out = jnp.where(lane_lo, scaled[0], scaled[1])
            o_ref[rows, (2 * h) * LANES:(2 * h + 1) * LANES] = out[0:tq].astype(BF16)
            o_ref[rows, (2 * h + 1) * LANES:(2 * h + 2) * LANES] = out[tq:2 * tq].astype(BF16)

    pl.when(aux_ref[1] > 0.5)(lambda: run(True))
    pl.when(aux_ref[1] <= 0.5)(lambda: run(False))


def _attn_call(sink, aux, bias, q, kz, vz, kcz, vcz, *, band):
    b, t, _ = q.shape
    lc = kcz.shape[1]
    tq = bias.shape[1] // 2
    tiles = 2 if t % (2 * tq) == 0 else 1
    full = lambda rows: pl.BlockSpec((None, rows, 4 * LANES), lambda i, j: (i, 0, 0))
    blk = pl.BlockSpec((None, tiles * tq, ATT_W), lambda i, j: (i, j, 0))
    return pl.pallas_call(
        functools.partial(_attn_kernel, band=band, t=t, tq=tq, tiles=tiles),
        grid=(b, t // (tiles * tq)),
        in_specs=[pl.BlockSpec(memory_space=pltpu.SMEM), pl.BlockSpec(memory_space=pltpu.SMEM),
                  pl.BlockSpec(bias.shape, lambda i, j: (0, 0, 0)),
                  blk, full(kz.shape[1]), full(vz.shape[1]), full(lc), full(lc)],
        out_specs=blk,
        out_shape=jax.ShapeDtypeStruct((b, t, ATT_W), BF16),
        compiler_params=_cparams(2),
        name="window_attention" if band else "context_attention",
    )(sink, aux, bias, q, kz, vz, kcz, vcz)


def _window_bias(tq):
    span = tq + 2 * WINDOW
    r = np.arange(2 * tq)[:, None] % tq
    c = np.arange(span)[None, :]
    cases = [np.where(np.abs(c - r - off) <= WINDOW, 0.0, NEG_INF) for off in (0, WINDOW, 2 * WINDOW)]
    return jnp.asarray(np.stack(cases), F32)


def _ret_kernel(lg_ref, lgl_ref, gn_ref, havg_ref,
                rq_ref, rk_ref, rv_ref, gate_ref, cq_ref, ck_ref, cv_ref, cgate_ref,
                o_ref, oc_ref,
                dec_ref, st_ref, of_ref, ob_ref, ocf_ref, ocb_ref, *, t, lc):
    c = RET_CHUNK

    @pl.when(pl.program_id(0) == 0)
    def _():
        ri = lax.broadcasted_iota(jnp.int32, (c, c), 0)
        ci = lax.broadcasted_iota(jnp.int32, (c, c), 1)
        diff = (ri - ci).astype(F32)
        for h in range(RET_HEADS):
            dec_ref[0, :, h * c:(h + 1) * c] = jnp.where(ri >= ci, jnp.exp(jnp.maximum(diff, 0.0) * lg_ref[0, h]), 0.0)
            dec_ref[1, :, h * c:(h + 1) * c] = jnp.where(ci >= ri, jnp.exp(jnp.maximum(-diff, 0.0) * lg_ref[1, h]), 0.0)

    idx = lax.broadcasted_iota(jnp.int32, (c, RET_W), 0).astype(F32)
    lgf = lgl_ref[0:1, :]
    lgb = lgl_ref[1:2, :]
    xi = (jnp.exp((idx + 1.0) * lgf), jnp.exp((c - idx) * lgb))
    zeta = (jnp.exp((c - 1.0 - idx) * lgf), jnp.exp(idx * lgb))
    cdec = (jnp.exp(c * lgf), jnp.exp(c * lgb))
    lane_head = lax.broadcasted_iota(jnp.int32, (1, RET_W), 1) // HEAD_DIM
    head_mask = [lane_head == h for h in range(RET_HEADS)]
    blockdiag = (lax.broadcasted_iota(jnp.int32, (RET_W, RET_W), 0) // HEAD_DIM
                 == lax.broadcasted_iota(jnp.int32, (RET_W, RET_W), 1) // HEAD_DIM)
    contract_last = (((1,), (1,)), ((), ()))
    contract_first = (((0,), (0,)), ((), ()))

    def by_head(v):
        zero = jnp.zeros_like(v)
        return jnp.concatenate([jnp.where(head_mask[h], v, zero) for h in range(RET_HEADS)], axis=0)

    st_ref[...] = jnp.zeros_like(st_ref)

    def run(n_chunks, q_ref_, k_ref_, v_ref_, out_refs):
        group = min(8, n_chunks)

        def body(i, carry):
            work = []
            for u in range(group):
                for direction in (0, 1):
                    idx = i * group + u
                    r0 = pl.multiple_of((idx if direction == 0 else n_chunks - 1 - idx) * c, c)
                    q, k, v = q_ref_[pl.ds(r0, c), :], k_ref_[pl.ds(r0, c), :], v_ref_[pl.ds(r0, c), :]
                    s = lax.dot_general(q, by_head(k), contract_last, preferred_element_type=F32)
                    kd = (k.astype(F32) * zeta[direction]).astype(BF16)
                    kv = lax.dot_general(kd, v, contract_first, preferred_element_type=F32)
                    work.append((direction, r0, q, v, s, kv))
            cross = []
            for direction, r0, q, v, s, kv in work:
                st = st_ref[direction]
                cross.append(jnp.dot(q, st.astype(BF16), preferred_element_type=F32) * xi[direction])
                st_ref[direction] = st * cdec[direction] + jnp.where(blockdiag, kv, 0.0)
            for (direction, r0, q, v, s, kv), cr in zip(work, cross):
                p = (s * dec_ref[direction]).astype(BF16)
                out_refs[direction][pl.ds(r0, c), :] = jnp.dot(p, by_head(v), preferred_element_type=F32) + cr
            return carry

        lax.fori_loop(0, n_chunks // group, body, 0)

    run(lc // c, cq_ref, ck_ref, cv_ref, (ocf_ref, ocb_ref))
    run(t // c, rq_ref, rk_ref, rv_ref, (of_ref, ob_ref))

    havg = havg_ref[...]
    gn_g = gn_ref[...]

    def standardize(o):
        mu = jnp.dot(o.astype(BF16), havg, preferred_element_type=F32)
        d = o - mu
        var = jnp.dot((d * d).astype(BF16), havg, preferred_element_type=F32)
        return d * lax.rsqrt(var + EPS)

    def finish(of, ob, gates):
        gf = gates[:, 0:RET_W].astype(F32)
        gb = gates[:, RET_W:2 * RET_W].astype(F32)
        return ((standardize(of) * _silu(gf) + standardize(ob) * _silu(gb)) * gn_g).astype(BF16)

    o_ref[...] = finish(of_ref[...], ob_ref[...], gate_ref[...])
    oc_ref[...] = finish(ocf_ref[...], ocb_ref[...], cgate_ref[...])


def _ret_call(lg, lgl, gn_g, havg, rq, rk, rv, gate, cq, ck, cv, cgate):
    b, t, _ = rq.shape
    lc = cq.shape[1]
    row = lambda i: (0, 0)
    seq = lambda rows, w: pl.BlockSpec((None, rows, w), lambda i: (i, 0, 0))
    return pl.pallas_call(
        functools.partial(_ret_kernel, t=t, lc=lc),
        grid=(b,),
        in_specs=[pl.BlockSpec(memory_space=pltpu.SMEM),
                  pl.BlockSpec((2, RET_W), row), pl.BlockSpec((1, RET_W), row), pl.BlockSpec((RET_W, RET_W), row),
                  seq(t, RET_W), seq(t, RET_W), seq(t, RET_W), seq(t, 2 * RET_W),
                  seq(lc, RET_W), seq(lc, RET_W), seq(lc, RET_W), seq(lc, 2 * RET_W)],
        out_specs=[seq(t, RET_W), seq(lc, RET_W)],
        out_shape=[jax.ShapeDtypeStruct((b, t, RET_W), BF16), jax.ShapeDtypeStruct((b, lc, RET_W), BF16)],
        scratch_shapes=[pltpu.VMEM((2, RET_CHUNK, RET_HEADS * RET_CHUNK), F32),
                        pltpu.VMEM((2, RET_W, RET_W), F32),
                        pltpu.VMEM((t, RET_W), F32), pltpu.VMEM((t, RET_W), F32),
                        pltpu.VMEM((lc, RET_W), F32), pltpu.VMEM((lc, RET_W), F32)],
        compiler_params=_cparams(1),
        name="retention",
    )(lg, lgl, gn_g, havg, rq, rk, rv, gate, cq, ck, cv, cgate)


def _fourier_kernel(c_ref, s_ref, flip_ref, u_ref, o_ref, *, t, blk, norm):
    half = t // 2
    n_blk = half // blk
    row0_wide = lax.broadcasted_iota(jnp.int32, (blk, 2 * FOUR_W), 0) == 0
    folded_c, folded_s = [], []
    for i in range(n_blk):
        rev = jnp.dot(flip_ref[...], u_ref[t - (i + 1) * blk:t - i * blk, :], preferred_element_type=F32)
        if i > 0:
            rev = jnp.where(row0_wide, u_ref[t - i * blk:t - i * blk + 1, :].astype(F32), rev)
        low = u_ref[i * blk:(i + 1) * blk, :].astype(F32)
        folded_c.append((low[:, 0:FOUR_W] + rev[:, 0:FOUR_W]).astype(BF16))
        folded_s.append((low[:, FOUR_W:] - rev[:, FOUR_W:]).astype(BF16))
    rows = c_ref.shape[0]
    sign = norm * (1.0 - 2.0 * (lax.broadcasted_iota(jnp.int32, (rows, 1), 0) % 2).astype(F32))
    a = (jnp.dot(c_ref[...], jnp.concatenate(folded_c, axis=0), preferred_element_type=F32)
         + sign * u_ref[half:half + 1, 0:FOUR_W].astype(F32))
    bs = jnp.dot(s_ref[...], jnp.concatenate(folded_s, axis=0), preferred_element_type=F32)
    o_ref[0:half, :] = (a[0:half] + bs[0:half]).astype(BF16)
    mid = (a[half:half + 1] + bs[half:half + 1]).astype(BF16)
    mirror = (a[0:half] - bs[0:half]).astype(BF16)
    row0 = lax.broadcasted_iota(jnp.int32, (blk, FOUR_W), 0) == 0
    for i in range(n_blk):
        src = mirror[(n_blk - 1 - i) * blk:(n_blk - i) * blk]
        rev = jnp.dot(flip_ref[...], src, preferred_element_type=F32).astype(BF16)
        first = mid if i == 0 else mirror[(n_blk - i) * blk:(n_blk - i) * blk + 1]
        o_ref[half + i * blk:half + (i + 1) * blk, :] = jnp.where(row0, first, rev)


def _fourier_call(cmat, smat_neg, flip, u):
    b, t, _ = u.shape
    rows = cmat.shape[0]
    blk = flip.shape[0]
    const = lambda *shape: pl.BlockSpec(shape, lambda i: (0,) * len(shape))
    return pl.pallas_call(
        functools.partial(_fourier_kernel, t=t, blk=blk, norm=float(t * (FOUR_W // FOURIER_GROUPS)) ** -0.5),
        grid=(b,),
        in_specs=[const(rows, t // 2), const(rows, t // 2), const(blk, blk),
                  pl.BlockSpec((None, t, 2 * FOUR_W), lambda i: (i, 0, 0))],
        out_specs=pl.BlockSpec((None, t, FOUR_W), lambda i: (i, 0, 0)),
        out_shape=jax.ShapeDtypeStruct((b, t, FOUR_W), BF16),
        compiler_params=_cparams(1),
        name="fourier_mix",
    )(cmat, smat_neg, flip, u)


def _mlp_kernel(x_ref, mod_ref, g_ref, att_ref, ret_ref, four_ref, wo_ref, w1_ref, w2_ref, o_ref, *, ff_chunk):
    m = (jnp.dot(att_ref[...], wo_ref[0:ATT_W, :], preferred_element_type=F32)
         + jnp.dot(ret_ref[...], wo_ref[ATT_W:ATT_W + RET_W, :], preferred_element_type=F32)
         + jnp.dot(four_ref[...], wo_ref[ATT_W + RET_W:, :], preferred_element_type=F32))
    x1 = x_ref[...] + mod_ref[2:3, :] * m
    ms = jnp.mean(x1 * x1, axis=-1, keepdims=True)
    hn = ((x1 * lax.rsqrt(ms + EPS)) * g_ref[...]) * (1.0 + mod_ref[4:5, :]) + mod_ref[3:4, :]
    hb = hn.astype(BF16)
    d_ff = w1_ref.shape[1]
    acc = jnp.zeros(x1.shape, F32)
    for c in range(d_ff // ff_chunk):
        hid = jnp.dot(hb, w1_ref[:, c * ff_chunk:(c + 1) * ff_chunk], preferred_element_type=F32)
        hid = jnp.square(jnp.maximum(hid, 0.0)).astype(BF16)
        acc = acc + jnp.dot(hid, w2_ref[c * ff_chunk:(c + 1) * ff_chunk, :], preferred_element_type=F32)
    o_ref[...] = x1 + mod_ref[5:6, :] * acc


def _mlp_call(layer, x, mod, mod_row, g2, att, ret, four, w_out, w1, w2, *, tm):
    b, t, d = x.shape
    d_ff = w1.shape[-1]
    tok = lambda w: pl.BlockSpec((None, tm, w), lambda i, j: (i, j, 0))
    once = dict(pipeline_mode=pl.Buffered(1))
    per_layer = lambda *shape, **kw: pl.BlockSpec((None,) + shape, lambda i, j: (layer,) + (0,) * len(shape), **kw)
    return pl.pallas_call(
        functools.partial(_mlp_kernel, ff_chunk=1024),
        grid=(b, t // tm),
        in_specs=[tok(d),
                  pl.BlockSpec((None, None, N_MOD, d), lambda i, j: (layer, mod_row(i), 0, 0)),
                  per_layer(1, d),
                  tok(ATT_W), tok(RET_W), tok(FOUR_W),
                  per_layer(d, d, **once),
                  per_layer(d, d_ff, **once),
                  per_layer(d_ff, d, **once)],
        out_specs=tok(d),
        out_shape=jax.ShapeDtypeStruct((b, t, d), F32),
        compiler_params=_cparams(2),
        name="out_projection_mlp",
    )(x, mod, g2, att, ret, four, w_out, w1, w2)


def _rope_tables(pos_groups, half):
    freqs = ROPE_BASE ** (-jnp.arange(half, dtype=F32) / half)
    cos, sin_next, sin_prev = [], [], []
    zeros = None
    for pos in pos_groups:
        ang = pos.astype(F32)[:, None] * freqs[None, :]
        c, s = jnp.cos(ang), jnp.sin(ang)
        zeros = jnp.zeros_like(s)
        cos += [c, c]
        sin_next += [-s, zeros]
        sin_prev += [zeros, s]
    reps = LANES // (2 * half * len(pos_groups))
    cat = lambda parts: jnp.tile(jnp.concatenate(parts, axis=1), (1, reps))
    return cat(cos), cat(sin_next), cat(sin_prev)


def _dft_tables(n):
    k = np.arange(n, dtype=np.int64)
    ang = 2.0 * np.pi * ((k[:, None] * k[None, :]) % n).astype(np.float64) / n
    return np.cos(ang), np.sin(ang)


def _block_diag(blocks):
    n = blocks.shape[-1]
    g = blocks.shape[-3]
    eye = jnp.eye(g, dtype=blocks.dtype)
    out = blocks[..., :, :, None, :] * eye[:, None, :, None]
    return out.reshape(blocks.shape[:-3] + (g * n, g * n))


def kernel(x, c, ctx, c_ctx, w_mod, b_mod, norm1_g, norm2_g, w_in, w_out, q_norm_g, k_norm_g, attn_sink,
           ret_decay_logit, ret_gn_g, fourier_w, w_ff1, w_ff2):
    b, s, d = x.shape
    lc = ctx.shape[1]
    depth = w_mod.shape[0]
    fdim = fourier_w.shape[-1]

    n_rows = -(-(b + 1) // 16) * 16
    cvec = jnp.zeros((n_rows, d), F32).at[:b].set(c).at[b].set(c_ctx)
    mod = _mod_call(cvec, w_mod, b_mod).reshape(depth, n_rows, N_MOD, d)

    pos = jnp.arange(s)
    tabs_lat = _rope_tables([pos // GRID_W, pos % GRID_W], HEAD_DIM // 4) + _rope_tables([pos], HEAD_DIM // 2)
    pair = 2 if b % 2 == 0 else 1
    paired = lambda v: v.reshape(b // pair, pair * lc, v.shape[-1])
    unpaired = lambda v: v.reshape(b, lc, v.shape[-1])
    tabs_ctx = tuple(jnp.zeros((pair * lc, LANES), F32) for _ in range(6))
    head_id = np.arange(256) // HEAD_DIM
    havg = jnp.asarray((head_id[:, None] == head_id[None, :]) / HEAD_DIM, BF16)
    c_ch, s_ch = _dft_tables(fdim)
    eye_g = np.eye(FOURIER_GROUPS)
    c_bd = jnp.asarray(np.kron(eye_g, c_ch), F32)
    s_bd = jnp.asarray(np.kron(eye_g, s_ch), F32)

    def pos_tables(n):
        cn, sn = _dft_tables(n)
        norm = 1.0 / np.sqrt(float(n) * fdim)
        rows = n // 2 + 16
        blk = min(256, n // 2)
        r = np.arange(blk)
        flip = ((r[:, None] + r[None, :]) == blk).astype(np.float32)
        cols = n // 2
        return (jnp.asarray(cn[:rows, :cols] * norm, F32).astype(BF16),
                jnp.asarray(-sn[:rows, :cols] * norm, F32).astype(BF16), jnp.asarray(flip, BF16))

    ftab_s = pos_tables(s)
    ftab_c = pos_tables(lc)
    win_bias = _window_bias(2 * BLOCK)
    logit_bound = (HEAD_DIM * Q_SCALE * 1.02) * jnp.max(jnp.abs(q_norm_g), axis=-1) * jnp.max(jnp.abs(k_norm_g), axis=-1)
    softmax_shift = jnp.maximum(logit_bound, jnp.max(attn_sink, axis=-1) * LOG2E)
    spread = logit_bound + softmax_shift
    attn_aux = jnp.stack([softmax_shift, (spread < 100.0).astype(F32)], axis=-1).astype(F32)

    wc, ws = _fourier_prep_call(_block_diag(fourier_w.astype(F32)), c_bd, s_bd)
    wcs = jnp.concatenate([wc, ws], axis=-1).astype(BF16)

    w_in_b = w_in.astype(BF16)
    w_out_b = w_out.astype(BF16)
    w1_b = w_ff1.astype(BF16)
    w2_b = w_ff2.astype(BF16)
    g1 = norm1_g.reshape(depth, 1, d)
    g2 = norm2_g.reshape(depth, 1, d)
    gq = jnp.tile(q_norm_g * Q_SCALE, (1, 256 // HEAD_DIM)).reshape(depth, 1, 256)
    gk = jnp.tile(k_norm_g, (1, LANES // HEAD_DIM)).reshape(depth, 1, LANES)
    log_gamma = -jax.nn.softplus(-ret_decay_logit.astype(F32))
    log_gamma_lanes = jnp.repeat(log_gamma, HEAD_DIM, axis=-1)

    lat_row = lambda i: i
    ctx_row = lambda i: b
    tm_lat = 1024 if s % 1024 == 0 else 512

    h = ctx
    for l in range(depth):
        need_ctx = l < depth - 1
        common = (g1, w_in_b, gq, gk, havg, wcs)
        q, kz, vz, rq, rk, rv, gate, u = _inproj_call(l, x, mod, lat_row, *common, tabs_lat, rope=True, tm=tm_lat)
        ctx_out = _inproj_call(l, paired(h), mod, ctx_row, *common, tabs_ctx, rope=False, tm=pair * lc)
        cq, ckz, cvz, crq, crk, crv, cgate, cu = (unpaired(o) for o in ctx_out)

        att = _attn_call(attn_sink[l], attn_aux[l], win_bias, q, kz, vz, ckz, cvz, band=True)
        ret, ret_c = _ret_call(log_gamma[l], log_gamma_lanes[l], ret_gn_g[l].reshape(1, RET_W), havg,
                               rq, rk, rv, gate, crq, crk, crv, cgate)
        four = _fourier_call(*ftab_s, u)
        x = _mlp_call(l, x, mod, lat_row, g2, att, ret, four, w_out_b, w1_b, w2_b, tm=tm_lat)
        if need_ctx:
            att_c = _attn_call(attn_sink[l], attn_aux[l], win_bias, cq, ckz, cvz, ckz, cvz, band=False)
            four_c = _fourier_call(*ftab_c, cu)
            h = unpaired(_mlp_call(l, paired(h), mod, ctx_row, g2, paired(att_c), paired(ret_c), paired(four_c),
                                   w_out_b, w1_b, w2_b, tm=pair * lc))
    return x
```

```python
import functools

import numpy as np
import jax
import jax.numpy as jnp
from jax import lax
from jax.experimental import pallas as pl
from jax.experimental.pallas import tpu as pltpu

F32 = jnp.float32
BF16 = jnp.bfloat16

HEAD_DIM = 64
LANES = 128
GRID_W = 64
N_MOD = 6
ATT_Q_HEADS = 8
ATT_KV_HEADS = 2
WINDOW = 128
BLOCK = 128
RET_HEADS = 4
RET_CHUNK = 128
FOURIER_GROUPS = 4
ROPE_BASE = 10000.0
EPS = 1e-6
NEG_INF = -1e30
VMEM_LIMIT_BYTES = 56 * 1024 * 1024

ATT_W = ATT_Q_HEADS * HEAD_DIM
KV_W = ATT_KV_HEADS * HEAD_DIM
RET_W = RET_HEADS * HEAD_DIM
FOUR_W = 256
O_Q, O_K, O_V = 0, ATT_W, ATT_W + KV_W
O_RQ = ATT_W + 2 * KV_W
O_RK, O_RV, O_GF, O_GB = O_RQ + RET_W, O_RQ + 2 * RET_W, O_RQ + 3 * RET_W, O_RQ + 4 * RET_W
O_FU = O_RQ + 5 * RET_W
PROJ_W = O_FU + FOUR_W
LOG2E = 1.4426950408889634
Q_SCALE = HEAD_DIM ** -0.5 * LOG2E


def _cparams(n_axes, flags=None):
    return pltpu.CompilerParams(dimension_semantics=("arbitrary",) * n_axes,
                                vmem_limit_bytes=VMEM_LIMIT_BYTES, flags=flags)


def _silu(v):
    return v / (1.0 + jnp.exp(-v))


def _mod_kernel(c_ref, w_ref, b_ref, o_ref):
    s = _silu(c_ref[...]).astype(BF16)
    o_ref[...] = jnp.dot(s, w_ref[...].astype(BF16), preferred_element_type=F32) + b_ref[...]


def _mod_call(cvec, w_mod, b_mod):
    depth, d, n = w_mod.shape
    r = cvec.shape[0]
    tn = 1536
    return pl.pallas_call(
        _mod_kernel,
        grid=(depth, n // tn),
        in_specs=[pl.BlockSpec((r, d), lambda l, j: (0, 0)),
                  pl.BlockSpec((None, d, tn), lambda l, j: (l, 0, j)),
                  pl.BlockSpec((None, 1, tn), lambda l, j: (l, 0, j))],
        out_specs=pl.BlockSpec((None, r, tn), lambda l, j: (l, 0, j)),
        out_shape=jax.ShapeDtypeStruct((depth, r, n), F32),
        compiler_params=_cparams(2),
        name="mod_vectors",
    )(cvec, w_mod, b_mod.reshape(depth, 1, n))


def _fourier_prep_kernel(w_ref, c_ref, s_ref, wc_ref, ws_ref):
    w = w_ref[...]
    wc_ref[...] = jnp.dot(c_ref[...], w, preferred_element_type=F32, precision=lax.Precision.HIGHEST)
    ws_ref[...] = jnp.dot(s_ref[...], w, preferred_element_type=F32, precision=lax.Precision.HIGHEST)


def _fourier_prep_call(w_bd, c_bd, s_bd):
    depth, n, _ = w_bd.shape
    spec_w = pl.BlockSpec((None, n, n), lambda l: (l, 0, 0))
    spec_t = pl.BlockSpec((n, n), lambda l: (0, 0))
    return pl.pallas_call(
        _fourier_prep_kernel,
        grid=(depth,),
        in_specs=[spec_w, spec_t, spec_t],
        out_specs=[spec_w, spec_w],
        out_shape=[jax.ShapeDtypeStruct((depth, n, n), F32)] * 2,
        compiler_params=_cparams(1),
        name="fourier_prep",
    )(w_bd, c_bd, s_bd)


def _rope(v, cos, sin_next, sin_prev, shift):
    return (v * cos + pltpu.roll(v, LANES - shift, 1) * sin_next + pltpu.roll(v, shift, 1) * sin_prev)


def _even_odd_variants(v, ones_lane=False):
    lane = lax.broadcasted_iota(jnp.int32, v.shape, 1)
    lo = lane < HEAD_DIM
    vr = pltpu.roll(v, HEAD_DIM, 1)
    pad_lo = jnp.where(lane == 0, 1.0, 0.0) if ones_lane else jnp.zeros_like(v)
    pad_hi = jnp.where(lane == HEAD_DIM, 1.0, 0.0) if ones_lane else jnp.zeros_like(v)
    return (jnp.where(lo, v, pad_hi), jnp.where(lo, pad_lo, vr), jnp.where(lo, vr, pad_hi), jnp.where(lo, pad_lo, v))


def _inproj_kernel(x_ref, mod_ref, g_ref, w_ref, gq_ref, gk_ref, havg_ref, wcs_ref,
                   ca_ref, san_ref, sap_ref, ct_ref, stn_ref, stp_ref,
                   q_ref, kz_ref, vz_ref, rq_ref, rk_ref, rv_ref, gate_ref, u_ref, *, rope):
    havg = havg_ref[...]

    def head_norm(v, gain):
        mean_sq = jnp.dot((v * v).astype(BF16), havg, preferred_element_type=F32)
        return v * lax.rsqrt(mean_sq + EPS) * gain

    x = x_ref[...]
    ms = jnp.mean(x * x, axis=-1, keepdims=True)
    a = (x * lax.rsqrt(ms + EPS)) * (g_ref[...] * (1.0 + mod_ref[1:2, :])) + mod_ref[0:1, :]
    proj = jnp.dot(a.astype(BF16), w_ref[...], preferred_element_type=F32)
    if rope:
        att_tab = (ca_ref[...], san_ref[...], sap_ref[...], HEAD_DIM // 4)
        ret_tab = (ct_ref[...], stn_ref[...], stp_ref[...], HEAD_DIM // 2)

    for j in range(ATT_W // 256):
        qn = head_norm(proj[:, O_Q + j * 256:O_Q + (j + 1) * 256], gq_ref[...])
        for c in range(2):
            col = qn[:, c * LANES:(c + 1) * LANES]
            if rope:
                col = _rope(col, *att_tab)
            lo = j * 256 + c * LANES
            q_ref[:, lo:lo + LANES] = col.astype(BF16)

    k = proj[:, O_K:O_V]
    mean_sq = jnp.dot((k * k).astype(BF16), havg[0:LANES, 0:LANES], preferred_element_type=F32)
    k = k * lax.rsqrt(mean_sq + EPS) * gk_ref[...]
    if rope:
        k = _rope(k, *att_tab)
    for i, var in enumerate(_even_odd_variants(k)):
        kz_ref[:, i * LANES:(i + 1) * LANES] = var.astype(BF16)
    for i, var in enumerate(_even_odd_variants(proj[:, O_V:O_RQ], ones_lane=True)):
        vz_ref[:, i * LANES:(i + 1) * LANES] = var.astype(BF16)

    for c in range(RET_W // LANES):
        qc = proj[:, O_RQ + c * LANES:O_RQ + (c + 1) * LANES]
        kc = proj[:, O_RK + c * LANES:O_RK + (c + 1) * LANES] * HEAD_DIM ** -0.5
        if rope:
            qc = _rope(qc, *ret_tab)
            kc = _rope(kc, *ret_tab)
        rq_ref[:, c * LANES:(c + 1) * LANES] = qc.astype(BF16)
        rk_ref[:, c * LANES:(c + 1) * LANES] = kc.astype(BF16)
    rv_ref[...] = proj[:, O_RV:O_GF].astype(BF16)
    gate_ref[...] = proj[:, O_GF:O_FU].astype(BF16)

    fu = proj[:, O_FU:PROJ_W].astype(BF16)
    u_ref[...] = jnp.dot(fu, wcs_ref[...], preferred_element_type=F32).astype(BF16)


def _inproj_call(layer, x, mod, mod_row, g1, w_in, gq, gk, havg, wcs, tabs, *, rope, tm):
    b, t, d = x.shape
    per_layer = lambda *shape: pl.BlockSpec((None,) + shape, lambda j, i: (layer,) + (0,) * len(shape))
    tab = pl.BlockSpec((tm, LANES), lambda j, i: (j, 0))
    tok = lambda w: pl.BlockSpec((None, tm, w), lambda j, i: (i, j, 0))
    out_w = (ATT_W, 4 * LANES, 4 * LANES, RET_W, RET_W, RET_W, 2 * RET_W, 2 * FOUR_W)
    return pl.pallas_call(
        functools.partial(_inproj_kernel, rope=rope),
        grid=(t // tm, b),
        in_specs=[tok(d),
                  pl.BlockSpec((None, None, N_MOD, d), lambda j, i: (layer, mod_row(i), 0, 0)),
                  per_layer(1, d),
                  per_layer(d, PROJ_W),
                  per_layer(1, 256),
                  per_layer(1, LANES),
                  pl.BlockSpec((256, 256), lambda j, i: (0, 0)),
                  per_layer(FOUR_W, 2 * FOUR_W),
                  tab, tab, tab, tab, tab, tab],
        out_specs=[tok(w) for w in out_w],
        out_shape=[jax.ShapeDtypeStruct((b, t, w), BF16) for w in out_w],
        compiler_params=_cparams(2),
        name="in_projection",
    )(x, mod, g1, w_in, gq, gk, havg, wcs, *tabs)


def _attn_kernel(sink_ref, aux_ref, bias_ref, q_ref, kz_ref, vz_ref, kcz_ref, vcz_ref, o_ref, *, band, t, tq, tiles):
    contract_last = (((1,), (1,)), ((), ()))
    span = tq + 2 * WINDOW
    starts, biases = [], []
    if band:
        last = t // tq - 1
        for sub in range(tiles):
            n = pl.program_id(1) * tiles + sub
            starts.append(pl.multiple_of(jnp.clip(n * tq - WINDOW, 0, t - span), BLOCK))
            biases.append(bias_ref[jnp.where(n == 0, 0, jnp.where(n == last, 2, 1))])
    upper = lax.broadcasted_iota(jnp.int32, (2 * tq, 1), 0) < tq
    lane_lo = lax.broadcasted_iota(jnp.int32, (2 * tq, LANES), 1) < HEAD_DIM
    units = [(sub, h) for sub in range(tiles) for h in range(ATT_KV_HEADS)]

    def run(use_bound):
        shifted = [bias - aux_ref[0] for bias in biases] if use_bound else None
        scores = {}
        for sub, h in units:
            rows = slice(sub * tq, (sub + 1) * tq)
            qq = jnp.concatenate([q_ref[rows, (2 * h) * LANES:(2 * h + 1) * LANES],
                                  q_ref[rows, (2 * h + 1) * LANES:(2 * h + 2) * LANES]], axis=0)
            for par in range(2):
                col = (2 * h + par) * LANES
                s_ctx = lax.dot_general(qq, kcz_ref[:, col:col + LANES], contract_last, preferred_element_type=F32)
                s_band = None
                if band:
                    s_band = lax.dot_general(qq, kz_ref[pl.ds(starts[sub], span), col:col + LANES], contract_last,
                                             preferred_element_type=F32)
                scores[sub, h, par] = (s_ctx, s_band)
        for sub, h in units:
            rows = slice(sub * tq, (sub + 1) * tq)
            scaled = []
            for par in range(2):
                col = (2 * h + par) * LANES
                s_ctx, s_band = scores[sub, h, par]
                sink = jnp.where(upper, sink_ref[4 * h + par], sink_ref[4 * h + 2 + par]) * LOG2E
                if use_bound:
                    m = aux_ref[0]
                    if band:
                        s_band = s_band + shifted[sub]
                else:
                    m = jnp.maximum(jnp.max(s_ctx, axis=-1, keepdims=True), sink)
                    if band:
                        s_band = s_band + biases[sub]
                        m = jnp.maximum(m, jnp.max(s_band, axis=-1, keepdims=True))
                        s_band = s_band - m
                acc = jnp.dot(jnp.exp2(s_ctx - m).astype(BF16), vcz_ref[:, col:col + LANES],
                              preferred_element_type=F32)
                if band:
                    acc = acc + jnp.dot(jnp.exp2(s_band).astype(BF16),
                                        vz_ref[pl.ds(starts[sub], span), col:col + LANES], preferred_element_type=F32)
                ones = HEAD_DIM if par == 0 else 0
                den = acc[:, ones:ones + 1] + jnp.exp2(sink - m)
                scaled.append(acc * (1.0 / den))
            out = jnp.where(lane_lo, scaled[0], scaled[1])
            o_ref[rows, (2 * h) * LANES:(2 * h + 1) * LANES] = out[0:tq].astype(BF16)
            o_ref[rows, (2 * h + 1) * LANES:(2 * h + 2) * LANES] = out[tq:2 * tq].astype(BF16)

    pl.when(aux_ref[1] > 0.5)(lambda: run(True))
    pl.when(aux_ref[1] <= 0.5)(lambda: run(False))


def _attn_call(sink, aux, bias, q, kz, vz, kcz, vcz, *, band):
    b, t, _ = q.shape
    lc = kcz.shape[1]
    tq = bias.shape[1] // 2
    tiles = 4 if t % (4 * tq) == 0 else (2 if t % (2 * tq) == 0 else 1)
    full = lambda rows: pl.BlockSpec((None, rows, 4 * LANES), lambda i, j: (i, 0, 0))
    blk = pl.BlockSpec((None, tiles * tq, ATT_W), lambda i, j: (i, j, 0))
    return pl.pallas_call(
        functools.partial(_attn_kernel, band=band, t=t, tq=tq, tiles=tiles),
        grid=(b, t // (tiles * tq)),
        in_specs=[pl.BlockSpec(memory_space=pltpu.SMEM), pl.BlockSpec(memory_space=pltpu.SMEM),
                  pl.BlockSpec(bias.shape, lambda i, j: (0, 0, 0)),
                  blk, full(kz.shape[1]), full(vz.shape[1]), full(lc), full(lc)],
        out_specs=blk,
        out_shape=jax.ShapeDtypeStruct((b, t, ATT_W), BF16),
        compiler_params=_cparams(2),
        name="window_attention" if band else "context_attention",
    )(sink, aux, bias, q, kz, vz, kcz, vcz)


def _window_bias(tq):
    span = tq + 2 * WINDOW
    r = np.arange(2 * tq)[:, None] % tq
    c = np.arange(span)[None, :]
    cases = [np.where(np.abs(c - r - off) <= WINDOW, 0.0, NEG_INF) for off in (0, WINDOW, 2 * WINDOW)]
    return jnp.asarray(np.stack(cases), F32)


def _ret_kernel(lg_ref, lgl_ref, gn_ref, havg_ref,
                rq_ref, rk_ref, rv_ref, gate_ref, cq_ref, ck_ref, cv_ref, cgate_ref,
                o_ref, oc_ref,
                dec_ref, st_ref, of_ref, ob_ref, ocf_ref, ocb_ref, *, t, lc):
    c = RET_CHUNK

    @pl.when(pl.program_id(0) == 0)
    def _():
        ri = lax.broadcasted_iota(jnp.int32, (c, c), 0)
        ci = lax.broadcasted_iota(jnp.int32, (c, c), 1)
        diff = (ri - ci).astype(F32)
        for h in range(RET_HEADS):
            dec_ref[0, :, h * c:(h + 1) * c] = jnp.where(ri >= ci, jnp.exp(jnp.maximum(diff, 0.0) * lg_ref[0, h]), 0.0)
            dec_ref[1, :, h * c:(h + 1) * c] = jnp.where(ci >= ri, jnp.exp(jnp.maximum(-diff, 0.0) * lg_ref[1, h]), 0.0)

    idx = lax.broadcasted_iota(jnp.int32, (c, RET_W), 0).astype(F32)
    lgf = lgl_ref[0:1, :]
    lgb = lgl_ref[1:2, :]
    xi = (jnp.exp((idx + 1.0) * lgf), jnp.exp((c - idx) * lgb))
    zeta = (jnp.exp((c - 1.0 - idx) * lgf), jnp.exp(idx * lgb))
    cdec = (jnp.exp(c * lgf), jnp.exp(c * lgb))
    lane_head = lax.broadcasted_iota(jnp.int32, (1, RET_W), 1) // HEAD_DIM
    head_mask = [lane_head == h for h in range(RET_HEADS)]
    blockdiag = (lax.broadcasted_iota(jnp.int32, (RET_W, RET_W), 0) // HEAD_DIM
                 == lax.broadcasted_iota(jnp.int32, (RET_W, RET_W), 1) // HEAD_DIM)
    contract_last = (((1,), (1,)), ((), ()))
    contract_first = (((0,), (0,)), ((), ()))

    def by_head(v):
        zero = jnp.zeros_like(v)
        return jnp.concatenate([jnp.where(head_mask[h], v, zero) for h in range(RET_HEADS)], axis=0)

    st_ref[...] = jnp.zeros_like(st_ref)

    def run(n_chunks, q_ref_, k_ref_, v_ref_, out_refs):
        group = min(8, n_chunks)

        def body(i, carry):
            work = []
            for u in range(group):
                for direction in (0, 1):
                    idx = i * group + u
                    r0 = pl.multiple_of((idx if direction == 0 else n_chunks - 1 - idx) * c, c)
                    q, k, v = q_ref_[pl.ds(r0, c), :], k_ref_[pl.ds(r0, c), :], v_ref_[pl.ds(r0, c), :]
                    s = lax.dot_general(q, by_head(k), contract_last, preferred_element_type=F32)
                    kd = (k.astype(F32) * zeta[direction]).astype(BF16)
                    kv = lax.dot_general(kd, v, contract_first, preferred_element_type=F32)
                    work.append((direction, r0, q, v, s, kv))
            cross = []
            for direction, r0, q, v, s, kv in work:
                st = st_ref[direction]
                cross.append(jnp.dot(q, st.astype(BF16), preferred_element_type=F32) * xi[direction])
                st_ref[direction] = st * cdec[direction] + jnp.where(blockdiag, kv, 0.0)
            for (direction, r0, q, v, s, kv), cr in zip(work, cross):
                p = (s * dec_ref[direction]).astype(BF16)
                out_refs[direction][pl.ds(r0, c), :] = jnp.dot(p, by_head(v), preferred_element_type=F32) + cr
            return carry

        lax.fori_loop(0, n_chunks // group, body, 0)

    run(lc // c, cq_ref, ck_ref, cv_ref, (ocf_ref, ocb_ref))
    run(t // c, rq_ref, rk_ref, rv_ref, (of_ref, ob_ref))

    havg = havg_ref[...]
    gn_g = gn_ref[...]

    def standardize(o):
        mu = jnp.dot(o.astype(BF16), havg, preferred_element_type=F32)
        d = o - mu
        var = jnp.dot((d * d).astype(BF16), havg, preferred_element_type=F32)
        return d * lax.rsqrt(var + EPS)

    def finish(of, ob, gates):
        gf = gates[:, 0:RET_W].astype(F32)
        gb = gates[:, RET_W:2 * RET_W].astype(F32)
        return ((standardize(of) * _silu(gf) + standardize(ob) * _silu(gb)) * gn_g).astype(BF16)

    o_ref[...] = finish(of_ref[...], ob_ref[...], gate_ref[...])
    oc_ref[...] = finish(ocf_ref[...], ocb_ref[...], cgate_ref[...])


def _ret_call(lg, lgl, gn_g, havg, rq, rk, rv, gate, cq, ck, cv, cgate):
    b, t, _ = rq.shape
    lc = cq.shape[1]
    row = lambda i: (0, 0)
    seq = lambda rows, w: pl.BlockSpec((None, rows, w), lambda i: (i, 0, 0))
    return pl.pallas_call(
        functools.partial(_ret_kernel, t=t, lc=lc),
        grid=(b,),
        in_specs=[pl.BlockSpec(memory_space=pltpu.SMEM),
                  pl.BlockSpec((2, RET_W), row), pl.BlockSpec((1, RET_W), row), pl.BlockSpec((RET_W, RET_W), row),
                  seq(t, RET_W), seq(t, RET_W), seq(t, RET_W), seq(t, 2 * RET_W),
                  seq(lc, RET_W), seq(lc, RET_W), seq(lc, RET_W), seq(lc, 2 * RET_W)],
        out_specs=[seq(t, RET_W), seq(lc, RET_W)],
        out_shape=[jax.ShapeDtypeStruct((b, t, RET_W), BF16), jax.ShapeDtypeStruct((b, lc, RET_W), BF16)],
        scratch_shapes=[pltpu.VMEM((2, RET_CHUNK, RET_HEADS * RET_CHUNK), F32),
                        pltpu.VMEM((2, RET_W, RET_W), F32),
                        pltpu.VMEM((t, RET_W), F32), pltpu.VMEM((t, RET_W), F32),
                        pltpu.VMEM((lc, RET_W), F32), pltpu.VMEM((lc, RET_W), F32)],
        compiler_params=_cparams(1),
        name="retention",
    )(lg, lgl, gn_g, havg, rq, rk, rv, gate, cq, ck, cv, cgate)


def _fourier_kernel(c_ref, s_ref, flip_ref, u_ref, o_ref, *, t, blk, norm):
    half = t // 2
    n_blk = half // blk
    row0_wide = lax.broadcasted_iota(jnp.int32, (blk, 2 * FOUR_W), 0) == 0
    folded_c, folded_s = [], []
    for i in range(n_blk):
        rev = jnp.dot(flip_ref[...], u_ref[t - (i + 1) * blk:t - i * blk, :], preferred_element_type=F32)
        if i > 0:
            rev = jnp.where(row0_wide, u_ref[t - i * blk:t - i * blk + 1, :].astype(F32), rev)
        low = u_ref[i * blk:(i + 1) * blk, :].astype(F32)
        folded_c.append((low[:, 0:FOUR_W] + rev[:, 0:FOUR_W]).astype(BF16))
        folded_s.append((low[:, FOUR_W:] - rev[:, FOUR_W:]).astype(BF16))
    rows = c_ref.shape[0]
    sign = norm * (1.0 - 2.0 * (lax.broadcasted_iota(jnp.int32, (rows, 1), 0) % 2).astype(F32))
    a = (jnp.dot(c_ref[...], jnp.concatenate(folded_c, axis=0), preferred_element_type=F32)
         + sign * u_ref[half:half + 1, 0:FOUR_W].astype(F32))
    bs = jnp.dot(s_ref[...], jnp.concatenate(folded_s, axis=0), preferred_element_type=F32)
    o_ref[0:half, :] = (a[0:half] + bs[0:half]).astype(BF16)
    mid = (a[half:half + 1] + bs[half:half + 1]).astype(BF16)
    mirror = (a[0:half] - bs[0:half]).astype(BF16)
    row0 = lax.broadcasted_iota(jnp.int32, (blk, FOUR_W), 0) == 0
    for i in range(n_blk):
        src = mirror[(n_blk - 1 - i) * blk:(n_blk - i) * blk]
        rev = jnp.dot(flip_ref[...], src, preferred_element_type=F32).astype(BF16)
        first = mid if i == 0 else mirror[(n_blk - i) * blk:(n_blk - i) * blk + 1]
        o_ref[half + i * blk:half + (i + 1) * blk, :] = jnp.where(row0, first, rev)


def _fourier_call(cmat, smat_neg, flip, u):
    b, t, _ = u.shape
    rows = cmat.shape[0]
    blk = flip.shape[0]
    const = lambda *shape: pl.BlockSpec(shape, lambda i: (0,) * len(shape))
    return pl.pallas_call(
        functools.partial(_fourier_kernel, t=t, blk=blk, norm=float(t * (FOUR_W // FOURIER_GROUPS)) ** -0.5),
        grid=(b,),
        in_specs=[const(rows, t // 2), const(rows, t // 2), const(blk, blk),
                  pl.BlockSpec((None, t, 2 * FOUR_W), lambda i: (i, 0, 0))],
        out_specs=pl.BlockSpec((None, t, FOUR_W), lambda i: (i, 0, 0)),
        out_shape=jax.ShapeDtypeStruct((b, t, FOUR_W), BF16),
        compiler_params=_cparams(1),
        name="fourier_mix",
    )(cmat, smat_neg, flip, u)


def _mlp_kernel(x_ref, mod_ref, g_ref, att_ref, ret_ref, four_ref, wo_ref, w1_ref, w2_ref, o_ref, *, ff_chunk):
    m = (jnp.dot(att_ref[...], wo_ref[0:ATT_W, :], preferred_element_type=F32)
         + jnp.dot(ret_ref[...], wo_ref[ATT_W:ATT_W + RET_W, :], preferred_element_type=F32)
         + jnp.dot(four_ref[...], wo_ref[ATT_W + RET_W:, :], preferred_element_type=F32))
    x1 = x_ref[...] + mod_ref[2:3, :] * m
    ms = jnp.mean(x1 * x1, axis=-1, keepdims=True)
    hn = ((x1 * lax.rsqrt(ms + EPS)) * g_ref[...]) * (1.0 + mod_ref[4:5, :]) + mod_ref[3:4, :]
    hb = hn.astype(BF16)
    d_ff = w1_ref.shape[1]
    acc = jnp.zeros(x1.shape, F32)
    for c in range(d_ff // ff_chunk):
        hid = jnp.dot(hb, w1_ref[:, c * ff_chunk:(c + 1) * ff_chunk], preferred_element_type=F32)
        hid = jnp.square(jnp.maximum(hid, 0.0)).astype(BF16)
        acc = acc + jnp.dot(hid, w2_ref[c * ff_chunk:(c + 1) * ff_chunk, :], preferred_element_type=F32)
    o_ref[...] = x1 + mod_ref[5:6, :] * acc


def _mlp_call(layer, x, mod, mod_row, g2, att, ret, four, w_out, w1, w2, *, tm):
    b, t, d = x.shape
    d_ff = w1.shape[-1]
    tok = lambda w: pl.BlockSpec((None, tm, w), lambda i, j: (i, j, 0))
    once = dict(pipeline_mode=pl.Buffered(1))
    per_layer = lambda *shape, **kw: pl.BlockSpec((None,) + shape, lambda i, j: (layer,) + (0,) * len(shape), **kw)
    return pl.pallas_call(
        functools.partial(_mlp_kernel, ff_chunk=1024),
        grid=(b, t // tm),
        in_specs=[tok(d),
                  pl.BlockSpec((None, None, N_MOD, d), lambda i, j: (layer, mod_row(i), 0, 0)),
                  per_layer(1, d),
                  tok(ATT_W), tok(RET_W), tok(FOUR_W),
                  per_layer(d, d, **once),
                  per_layer(d, d_ff, **once),
                  per_layer(d_ff, d, **once)],
        out_specs=tok(d),
        out_shape=jax.ShapeDtypeStruct((b, t, d), F32),
        compiler_params=_cparams(2),
        name="out_projection_mlp",
    )(x, mod, g2, att, ret, four, w_out, w1, w2)


def _rope_tables(pos_groups, half):
    freqs = ROPE_BASE ** (-jnp.arange(half, dtype=F32) / half)
    cos, sin_next, sin_prev = [], [], []
    zeros = None
    for pos in pos_groups:
        ang = pos.astype(F32)[:, None] * freqs[None, :]
        c, s = jnp.cos(ang), jnp.sin(ang)
        zeros = jnp.zeros_like(s)
        cos += [c, c]
        sin_next += [-s, zeros]
        sin_prev += [zeros, s]
    reps = LANES // (2 * half * len(pos_groups))
    cat = lambda parts: jnp.tile(jnp.concatenate(parts, axis=1), (1, reps))
    return cat(cos), cat(sin_next), cat(sin_prev)


def _dft_tables(n):
    k = np.arange(n, dtype=np.int64)
    ang = 2.0 * np.pi * ((k[:, None] * k[None, :]) % n).astype(np.float64) / n
    return np.cos(ang), np.sin(ang)


def _block_diag(blocks):
    n = blocks.shape[-1]
    g = blocks.shape[-3]
    eye = jnp.eye(g, dtype=blocks.dtype)
    out = blocks[..., :, :, None, :] * eye[:, None, :, None]
    return out.reshape(blocks.shape[:-3] + (g * n, g * n))


def kernel(x, c, ctx, c_ctx, w_mod, b_mod, norm1_g, norm2_g, w_in, w_out, q_norm_g, k_norm_g, attn_sink,
           ret_decay_logit, ret_gn_g, fourier_w, w_ff1, w_ff2):
    b, s, d = x.shape
    lc = ctx.shape[1]
    depth = w_mod.shape[0]
    fdim = fourier_w.shape[-1]

    n_rows = -(-(b + 1) // 16) * 16
    cvec = jnp.zeros((n_rows, d), F32).at[:b].set(c).at[b].set(c_ctx)
    mod = _mod_call(cvec, w_mod, b_mod).reshape(depth, n_rows, N_MOD, d)

    pos = jnp.arange(s)
    tabs_lat = _rope_tables([pos // GRID_W, pos % GRID_W], HEAD_DIM // 4) + _rope_tables([pos], HEAD_DIM // 2)
    pair = 2 if b % 2 == 0 else 1
    paired = lambda v: v.reshape(b // pair, pair * lc, v.shape[-1])
    unpaired = lambda v: v.reshape(b, lc, v.shape[-1])
    tabs_ctx = tuple(jnp.zeros((pair * lc, LANES), F32) for _ in range(6))
    head_id = np.arange(256) // HEAD_DIM
    havg = jnp.asarray((head_id[:, None] == head_id[None, :]) / HEAD_DIM, BF16)
    c_ch, s_ch = _dft_tables(fdim)
    eye_g = np.eye(FOURIER_GROUPS)
    c_bd = jnp.asarray(np.kron(eye_g, c_ch), F32)
    s_bd = jnp.asarray(np.kron(eye_g, s_ch), F32)

    def pos_tables(n):
        cn, sn = _dft_tables(n)
        norm = 1.0 / np.sqrt(float(n) * fdim)
        rows = n // 2 + 16
        blk = min(256, n // 2)
        r = np.arange(blk)
        flip = ((r[:, None] + r[None, :]) == blk).astype(np.float32)
        cols = n // 2
        return (jnp.asarray(cn[:rows, :cols] * norm, F32).astype(BF16),
                jnp.asarray(-sn[:rows, :cols] * norm, F32).astype(BF16), jnp.asarray(flip, BF16))

    ftab_s = pos_tables(s)
    ftab_c = pos_tables(lc)
    win_bias = _window_bias(2 * BLOCK)
    logit_bound = (HEAD_DIM * Q_SCALE * 1.02) * jnp.max(jnp.abs(q_norm_g), axis=-1) * jnp.max(jnp.abs(k_norm_g), axis=-1)
    softmax_shift = jnp.maximum(logit_bound, jnp.max(attn_sink, axis=-1) * LOG2E)
    spread = logit_bound + softmax_shift
    attn_aux = jnp.stack([softmax_shift, (spread < 100.0).astype(F32)], axis=-1).astype(F32)

    wc, ws = _fourier_prep_call(_block_diag(fourier_w.astype(F32)), c_bd, s_bd)
    wcs = jnp.concatenate([wc, ws], axis=-1).astype(BF16)

    w_in_b = w_in.astype(BF16)
    w_out_b = w_out.astype(BF16)
    w1_b = w_ff1.astype(BF16)
    w2_b = w_ff2.astype(BF16)
    g1 = norm1_g.reshape(depth, 1, d)
    g2 = norm2_g.reshape(depth, 1, d)
    gq = jnp.tile(q_norm_g * Q_SCALE, (1, 256 // HEAD_DIM)).reshape(depth, 1, 256)
    gk = jnp.tile(k_norm_g, (1, LANES // HEAD_DIM)).reshape(depth, 1, LANES)
    log_gamma = -jax.nn.softplus(-ret_decay_logit.astype(F32))
    log_gamma_lanes = jnp.repeat(log_gamma, HEAD_DIM, axis=-1)

    lat_row = lambda i: i
    ctx_row = lambda i: b
    tm_lat = 1024 if s % 1024 == 0 else 512

    h = ctx
    for l in range(depth):
        need_ctx = l < depth - 1
        common = (g1, w_in_b, gq, gk, havg, wcs)
        q, kz, vz, rq, rk, rv, gate, u = _inproj_call(l, x, mod, lat_row, *common, tabs_lat, rope=True, tm=tm_lat)
        ctx_out = _inproj_call(l, paired(h), mod, ctx_row, *common, tabs_ctx, rope=False, tm=pair * lc)
        cq, ckz, cvz, crq, crk, crv, cgate, cu = (unpaired(o) for o in ctx_out)

        att = _attn_call(attn_sink[l], attn_aux[l], win_bias, q, kz, vz, ckz, cvz, band=True)
        ret, ret_c = _ret_call(log_gamma[l], log_gamma_lanes[l], ret_gn_g[l].reshape(1, RET_W), havg,
                               rq, rk, rv, gate, crq, crk, crv, cgate)
        four = _fourier_call(*ftab_s, u)
        x = _mlp_call(l, x, mod, lat_row, g2, att, ret, four, w_out_b, w1_b, w2_b, tm=tm_lat)
        if need_ctx:
            att_c = _attn_call(attn_sink[l], attn_aux[l], win_bias, cq, ckz, cvz, ckz, cvz, band=False)
            four_c = _fourier_call(*ftab_c, cu)
            h = unpaired(_mlp_call(l, paired(h), mod, ctx_row, g2, paired(att_c), paired(ret_c), paired(four_c),
                                   w_out_b, w1_b, w2_b, tm=pair * lc))
    return x
```

```python
import functools

import numpy as np
import jax
import jax.numpy as jnp
from jax import lax
from jax.experimental import pallas as pl
from jax.experimental.pallas import tpu as pltpu

F32 = jnp.float32
BF16 = jnp.bfloat16

HEAD_DIM = 64
LANES = 128
GRID_W = 64
N_MOD = 6
ATT_Q_HEADS = 8
ATT_KV_HEADS = 2
WINDOW = 128
BLOCK = 128
RET_HEADS = 4
RET_CHUNK = 128
FOURIER_GROUPS = 4
ROPE_BASE = 10000.0
EPS = 1e-6
NEG_INF = -1e30
VMEM_LIMIT_BYTES = 56 * 1024 * 1024

ATT_W = ATT_Q_HEADS * HEAD_DIM
KV_W = ATT_KV_HEADS * HEAD_DIM
RET_W = RET_HEADS * HEAD_DIM
FOUR_W = 256
O_Q, O_K, O_V = 0, ATT_W, ATT_W + KV_W
O_RQ = ATT_W + 2 * KV_W
O_RK, O_RV, O_GF, O_GB = O_RQ + RET_W, O_RQ + 2 * RET_W, O_RQ + 3 * RET_W, O_RQ + 4 * RET_W
O_FU = O_RQ + 5 * RET_W
PROJ_W = O_FU + FOUR_W
LOG2E = 1.4426950408889634
Q_SCALE = HEAD_DIM ** -0.5 * LOG2E


def _cparams(n_axes, flags=None, fuse_inputs=None):
    return pltpu.CompilerParams(dimension_semantics=("arbitrary",) * n_axes,
                                vmem_limit_bytes=VMEM_LIMIT_BYTES, flags=flags, allow_input_fusion=fuse_inputs)


def _silu(v):
    return v / (1.0 + jnp.exp(-v))


def _mod_kernel(c_ref, w_ref, b_ref, o_ref):
    s = _silu(c_ref[...]).astype(BF16)
    o_ref[...] = jnp.dot(s, w_ref[...].astype(BF16), preferred_element_type=F32) + b_ref[...]


def _mod_call(cvec, w_mod, b_mod):
    depth, d, n = w_mod.shape
    r = cvec.shape[0]
    tn = 1536
    return pl.pallas_call(
        _mod_kernel,
        grid=(depth, n // tn),
        in_specs=[pl.BlockSpec((r, d), lambda l, j: (0, 0)),
                  pl.BlockSpec((None, d, tn), lambda l, j: (l, 0, j)),
                  pl.BlockSpec((None, 1, tn), lambda l, j: (l, 0, j))],
        out_specs=pl.BlockSpec((None, r, tn), lambda l, j: (l, 0, j)),
        out_shape=jax.ShapeDtypeStruct((depth, r, n), F32),
        compiler_params=_cparams(2),
        name="mod_vectors",
    )(cvec, w_mod, b_mod.reshape(depth, 1, n))


def _fourier_prep_kernel(w_ref, c_ref, s_ref, wc_ref, ws_ref):
    w = w_ref[...]
    wc_ref[...] = jnp.dot(c_ref[...], w, preferred_element_type=F32, precision=lax.Precision.HIGHEST)
    ws_ref[...] = jnp.dot(s_ref[...], w, preferred_element_type=F32, precision=lax.Precision.HIGHEST)


def _fourier_prep_call(w_bd, c_bd, s_bd):
    depth, n, _ = w_bd.shape
    spec_w = pl.BlockSpec((None, n, n), lambda l: (l, 0, 0))
    spec_t = pl.BlockSpec((n, n), lambda l: (0, 0))
    return pl.pallas_call(
        _fourier_prep_kernel,
        grid=(depth,),
        in_specs=[spec_w, spec_t, spec_t],
        out_specs=[spec_w, spec_w],
        out_shape=[jax.ShapeDtypeStruct((depth, n, n), F32)] * 2,
        compiler_params=_cparams(1),
        name="fourier_prep",
    )(w_bd, c_bd, s_bd)


def _rope(v, cos, sin_next, sin_prev, shift):
    return (v * cos + pltpu.roll(v, LANES - shift, 1) * sin_next + pltpu.roll(v, shift, 1) * sin_prev)


def _even_odd_variants(v, ones_lane=False):
    lane = lax.broadcasted_iota(jnp.int32, v.shape, 1)
    lo = lane < HEAD_DIM
    vr = pltpu.roll(v, HEAD_DIM, 1)
    pad_lo = jnp.where(lane == 0, 1.0, 0.0) if ones_lane else jnp.zeros_like(v)
    pad_hi = jnp.where(lane == HEAD_DIM, 1.0, 0.0) if ones_lane else jnp.zeros_like(v)
    return (jnp.where(lo, v, pad_hi), jnp.where(lo, pad_lo, vr), jnp.where(lo, vr, pad_hi), jnp.where(lo, pad_lo, v))


def _inproj_kernel(x_ref, mod_ref, g_ref, w_ref, gq_ref, gk_ref, havg_ref, wcs_ref,
                   ca_ref, san_ref, sap_ref, ct_ref, stn_ref, stp_ref,
                   q_ref, kz_ref, vz_ref, rq_ref, rk_ref, rv_ref, gate_ref, u_ref, *, rope):
    havg = havg_ref[...]

    def head_norm(v, gain):
        mean_sq = jnp.dot((v * v).astype(BF16), havg, preferred_element_type=F32)
        return v * lax.rsqrt(mean_sq + EPS) * gain

    x = x_ref[...]
    ms = jnp.mean(x * x, axis=-1, keepdims=True)
    a = (x * lax.rsqrt(ms + EPS)) * (g_ref[...] * (1.0 + mod_ref[1:2, :])) + mod_ref[0:1, :]
    proj = jnp.dot(a.astype(BF16), w_ref[...], preferred_element_type=F32)
    if rope:
        att_tab = (ca_ref[...], san_ref[...], sap_ref[...], HEAD_DIM // 4)
        ret_tab = (ct_ref[...], stn_ref[...], stp_ref[...], HEAD_DIM // 2)

    for j in range(ATT_W // 256):
        qn = head_norm(proj[:, O_Q + j * 256:O_Q + (j + 1) * 256], gq_ref[...])
        for c in range(2):
            col = qn[:, c * LANES:(c + 1) * LANES]
            if rope:
                col = _rope(col, *att_tab)
            lo = j * 256 + c * LANES
            q_ref[:, lo:lo + LANES] = col.astype(BF16)

    k = proj[:, O_K:O_V]
    mean_sq = jnp.dot((k * k).astype(BF16), havg[0:LANES, 0:LANES], preferred_element_type=F32)
    k = k * lax.rsqrt(mean_sq + EPS) * gk_ref[...]
    if rope:
        k = _rope(k, *att_tab)
    for i, var in enumerate(_even_odd_variants(k)):
        kz_ref[:, i * LANES:(i + 1) * LANES] = var.astype(BF16)
    for i, var in enumerate(_even_odd_variants(proj[:, O_V:O_RQ], ones_lane=True)):
        vz_ref[:, i * LANES:(i + 1) * LANES] = var.astype(BF16)

    for c in range(RET_W // LANES):
        qc = proj[:, O_RQ + c * LANES:O_RQ + (c + 1) * LANES]
        kc = proj[:, O_RK + c * LANES:O_RK + (c + 1) * LANES] * HEAD_DIM ** -0.5
        if rope:
            qc = _rope(qc, *ret_tab)
            kc = _rope(kc, *ret_tab)
        rq_ref[:, c * LANES:(c + 1) * LANES] = qc.astype(BF16)
        rk_ref[:, c * LANES:(c + 1) * LANES] = kc.astype(BF16)
    rv_ref[...] = proj[:, O_RV:O_GF].astype(BF16)
    gate_ref[...] = proj[:, O_GF:O_FU].astype(BF16)

    fu = proj[:, O_FU:PROJ_W].astype(BF16)
    u_ref[...] = jnp.dot(fu, wcs_ref[...], preferred_element_type=F32).astype(BF16)


def _inproj_call(layer, x, mod, mod_row, g1, w_in, gq, gk, havg, wcs, tabs, *, rope, tm):
    b, t, d = x.shape
    per_layer = lambda *shape: pl.BlockSpec((None,) + shape, lambda j, i: (layer,) + (0,) * len(shape))
    tab = pl.BlockSpec((tm, LANES), lambda j, i: (j, 0))
    tok = lambda w: pl.BlockSpec((None, tm, w), lambda j, i: (i, j, 0))
    out_w = (ATT_W, 4 * LANES, 4 * LANES, RET_W, RET_W, RET_W, 2 * RET_W, 2 * FOUR_W)
    return pl.pallas_call(
        functools.partial(_inproj_kernel, rope=rope),
        grid=(t // tm, b),
        in_specs=[tok(d),
                  pl.BlockSpec((None, None, N_MOD, d), lambda j, i: (layer, mod_row(i), 0, 0)),
                  per_layer(1, d),
                  per_layer(d, PROJ_W),
                  per_layer(1, 256),
                  per_layer(1, LANES),
                  pl.BlockSpec((256, 256), lambda j, i: (0, 0)),
                  per_layer(FOUR_W, 2 * FOUR_W),
                  tab, tab, tab, tab, tab, tab],
        out_specs=[tok(w) for w in out_w],
        out_shape=[jax.ShapeDtypeStruct((b, t, w), BF16) for w in out_w],
        compiler_params=_cparams(2, fuse_inputs=[i == 3 for i in range(14)]),
        name="in_projection",
    )(x, mod, g1, w_in, gq, gk, havg, wcs, *tabs)


def _attn_kernel(sink_ref, aux_ref, bias_ref, q_ref, kz_ref, vz_ref, kcz_ref, vcz_ref, o_ref, *, band, t, tq, tiles):
    contract_last = (((1,), (1,)), ((), ()))
    span = tq + 2 * WINDOW
    starts, biases = [], []
    if band:
        last = t // tq - 1
        for sub in range(tiles):
            n = pl.program_id(1) * tiles + sub
            starts.append(pl.multiple_of(jnp.clip(n * tq - WINDOW, 0, t - span), BLOCK))
            biases.append(bias_ref[jnp.where(n == 0, 0, jnp.where(n == last, 2, 1))])
    upper = lax.broadcasted_iota(jnp.int32, (2 * tq, 1), 0) < tq
    lane_lo = lax.broadcasted_iota(jnp.int32, (2 * tq, LANES), 1) < HEAD_DIM
    units = [(sub, h) for sub in range(tiles) for h in range(ATT_KV_HEADS)]

    def run(use_bound):
        shifted = [bias - aux_ref[0] for bias in biases] if use_bound else None
        scores = {}
        for sub, h in units:
            rows = slice(sub * tq, (sub + 1) * tq)
            qq = jnp.concatenate([q_ref[rows, (2 * h) * LANES:(2 * h + 1) * LANES],
                                  q_ref[rows, (2 * h + 1) * LANES:(2 * h + 2) * LANES]], axis=0)
            for par in range(2):
                col = (2 * h + par) * LANES
                s_ctx = lax.dot_general(qq, kcz_ref[:, col:col + LANES], contract_last, preferred_element_type=F32)
                s_band = None
                if band:
                    s_band = lax.dot_general(qq, kz_ref[pl.ds(starts[sub], span), col:col + LANES], contract_last,
                                             preferred_element_type=F32)
                scores[sub, h, par] = (s_ctx, s_band)
        for sub, h in units:
            rows = slice(sub * tq, (sub + 1) * tq)
            scaled = []
            for par in range(2):
                col = (2 * h + par) * LANES
                s_ctx, s_band = scores[sub, h, par]
                sink = jnp.where(upper, sink_ref[4 * h + par], sink_ref[4 * h + 2 + par]) * LOG2E
                if use_bound:
                    m = aux_ref[0]
                    if band:
                        s_band = s_band + shifted[sub]
                else:
                    m = jnp.maximum(jnp.max(s_ctx, axis=-1, keepdims=True), sink)
                    if band:
                        s_band = s_band + biases[sub]
                        m = jnp.maximum(m, jnp.max(s_band, axis=-1, keepdims=True))
                        s_band = s_band - m
                acc = jnp.dot(jnp.exp2(s_ctx - m).astype(BF16), vcz_ref[:, col:col + LANES],
                              preferred_element_type=F32)
                if band:
                    acc = acc + jnp.dot(jnp.exp2(s_band).astype(BF16),
                                        vz_ref[pl.ds(starts[sub], span), col:col + LANES], preferred_element_type=F32)
                ones = HEAD_DIM if par == 0 else 0
                den = acc[:, ones:ones + 1] + jnp.exp2(sink - m)
                scaled.append(acc * (1.0 / den))
            out = jnp.where(lane_lo, scaled[0], scaled[1])
            o_ref[rows, (2 * h) * LANES:(2 * h + 1) * LANES] = out[0:tq].astype(BF16)
            o_ref[rows, (2 * h + 1) * LANES:(2 * h + 2) * LANES] = out[tq:2 * tq].astype(BF16)

    pl.when(aux_ref[1] > 0.5)(lambda: run(True))
    pl.when(aux_ref[1] <= 0.5)(lambda: run(False))


def _attn_call(sink, aux, bias, q, kz, vz, kcz, vcz, *, band):
    b, t, _ = q.shape
    lc = kcz.shape[1]
    tq = bias.shape[1] // 2
    tiles = 4 if t % (4 * tq) == 0 else (2 if t % (2 * tq) == 0 else 1)
    full = lambda rows: pl.BlockSpec((None, rows, 4 * LANES), lambda i, j: (i, 0, 0))
    blk = pl.BlockSpec((None, tiles * tq, ATT_W), lambda i, j: (i, j, 0))
    return pl.pallas_call(
        functools.partial(_attn_kernel, band=band, t=t, tq=tq, tiles=tiles),
        grid=(b, t // (tiles * tq)),
        in_specs=[pl.BlockSpec(memory_space=pltpu.SMEM), pl.BlockSpec(memory_space=pltpu.SMEM),
                  pl.BlockSpec(bias.shape, lambda i, j: (0, 0, 0)),
                  blk, full(kz.shape[1]), full(vz.shape[1]), full(lc), full(lc)],
        out_specs=blk,
        out_shape=jax.ShapeDtypeStruct((b, t, ATT_W), BF16),
        compiler_params=_cparams(2),
        name="window_attention" if band else "context_attention",
    )(sink, aux, bias, q, kz, vz, kcz, vcz)


def _window_bias(tq):
    span = tq + 2 * WINDOW
    r = np.arange(2 * tq)[:, None] % tq
    c = np.arange(span)[None, :]
    cases = [np.where(np.abs(c - r - off) <= WINDOW, 0.0, NEG_INF) for off in (0, WINDOW, 2 * WINDOW)]
    return jnp.asarray(np.stack(cases), F32)


def _ret_kernel(lg_ref, lgl_ref, gn_ref, havg_ref,
                rq_ref, rk_ref, rv_ref, gate_ref, cq_ref, ck_ref, cv_ref, cgate_ref,
                o_ref, oc_ref,
                dec_ref, st_ref, of_ref, ob_ref, ocf_ref, ocb_ref, *, t, lc):
    c = RET_CHUNK

    @pl.when(pl.program_id(0) == 0)
    def _():
        ri = lax.broadcasted_iota(jnp.int32, (c, c), 0)
        ci = lax.broadcasted_iota(jnp.int32, (c, c), 1)
        diff = (ri - ci).astype(F32)
        for h in range(RET_HEADS):
            dec_ref[0, :, h * c:(h + 1) * c] = jnp.where(ri >= ci, jnp.exp(jnp.maximum(diff, 0.0) * lg_ref[0, h]), 0.0)
            dec_ref[1, :, h * c:(h + 1) * c] = jnp.where(ci >= ri, jnp.exp(jnp.maximum(-diff, 0.0) * lg_ref[1, h]), 0.0)

    idx = lax.broadcasted_iota(jnp.int32, (c, RET_W), 0).astype(F32)
    lgf = lgl_ref[0:1, :]
    lgb = lgl_ref[1:2, :]
    xi = (jnp.exp((idx + 1.0) * lgf), jnp.exp((c - idx) * lgb))
    zeta = (jnp.exp((c - 1.0 - idx) * lgf), jnp.exp(idx * lgb))
    cdec = (jnp.exp(c * lgf), jnp.exp(c * lgb))
    lane_head = lax.broadcasted_iota(jnp.int32, (1, RET_W), 1) // HEAD_DIM
    head_mask = [lane_head == h for h in range(RET_HEADS)]
    blockdiag = (lax.broadcasted_iota(jnp.int32, (RET_W, RET_W), 0) // HEAD_DIM
                 == lax.broadcasted_iota(jnp.int32, (RET_W, RET_W), 1) // HEAD_DIM)
    contract_last = (((1,), (1,)), ((), ()))
    contract_first = (((0,), (0,)), ((), ()))

    def by_head(v):
        zero = jnp.zeros_like(v)
        return jnp.concatenate([jnp.where(head_mask[h], v, zero) for h in range(RET_HEADS)], axis=0)

    st_ref[...] = jnp.zeros_like(st_ref)

    def run(n_chunks, q_ref_, k_ref_, v_ref_, out_refs):
        group = min(8, n_chunks)

        def body(i, carry):
            work = []
            for u in range(group):
                for direction in (0, 1):
                    idx = i * group + u
                    r0 = pl.multiple_of((idx if direction == 0 else n_chunks - 1 - idx) * c, c)
                    q, k, v = q_ref_[pl.ds(r0, c), :], k_ref_[pl.ds(r0, c), :], v_ref_[pl.ds(r0, c), :]
                    s = lax.dot_general(q, by_head(k), contract_last, preferred_element_type=F32)
                    kd = (k.astype(F32) * zeta[direction]).astype(BF16)
                    kv = lax.dot_general(kd, v, contract_first, preferred_element_type=F32)
                    work.append((direction, r0, q, v, s, kv))
            cross = []
            for direction, r0, q, v, s, kv in work:
                st = st_ref[direction]
                cross.append(jnp.dot(q, st.astype(BF16), preferred_element_type=F32) * xi[direction])
                st_ref[direction] = st * cdec[direction] + jnp.where(blockdiag, kv, 0.0)
            for (direction, r0, q, v, s, kv), cr in zip(work, cross):
                p = (s * dec_ref[direction]).astype(BF16)
                out_refs[direction][pl.ds(r0, c), :] = jnp.dot(p, by_head(v), preferred_element_type=F32) + cr
            return carry

        lax.fori_loop(0, n_chunks // group, body, 0)

    run(lc // c, cq_ref, ck_ref, cv_ref, (ocf_ref, ocb_ref))
    run(t // c, rq_ref, rk_ref, rv_ref, (of_ref, ob_ref))

    havg = havg_ref[...]
    gn_g = gn_ref[...]

    def standardize(o):
        mu = jnp.dot(o.astype(BF16), havg, preferred_element_type=F32)
        d = o - mu
        var = jnp.dot((d * d).astype(BF16), havg, preferred_element_type=F32)
        return d * lax.rsqrt(var + EPS)

    def finish(of, ob, gates):
        gf = gates[:, 0:RET_W].astype(F32)
        gb = gates[:, RET_W:2 * RET_W].astype(F32)
        return ((standardize(of) * _silu(gf) + standardize(ob) * _silu(gb)) * gn_g).astype(BF16)

    o_ref[...] = finish(of_ref[...], ob_ref[...], gate_ref[...])
    oc_ref[...] = finish(ocf_ref[...], ocb_ref[...], cgate_ref[...])


def _ret_call(lg, lgl, gn_g, havg, rq, rk, rv, gate, cq, ck, cv, cgate):
    b, t, _ = rq.shape
    lc = cq.shape[1]
    row = lambda i: (0, 0)
    seq = lambda rows, w: pl.BlockSpec((None, rows, w), lambda i: (i, 0, 0))
    return pl.pallas_call(
        functools.partial(_ret_kernel, t=t, lc=lc),
        grid=(b,),
        in_specs=[pl.BlockSpec(memory_space=pltpu.SMEM),
                  pl.BlockSpec((2, RET_W), row), pl.BlockSpec((1, RET_W), row), pl.BlockSpec((RET_W, RET_W), row),
                  seq(t, RET_W), seq(t, RET_W), seq(t, RET_W), seq(t, 2 * RET_W),
                  seq(lc, RET_W), seq(lc, RET_W), seq(lc, RET_W), seq(lc, 2 * RET_W)],
        out_specs=[seq(t, RET_W), seq(lc, RET_W)],
        out_shape=[jax.ShapeDtypeStruct((b, t, RET_W), BF16), jax.ShapeDtypeStruct((b, lc, RET_W), BF16)],
        scratch_shapes=[pltpu.VMEM((2, RET_CHUNK, RET_HEADS * RET_CHUNK), F32),
                        pltpu.VMEM((2, RET_W, RET_W), F32),
                        pltpu.VMEM((t, RET_W), F32), pltpu.VMEM((t, RET_W), F32),
                        pltpu.VMEM((lc, RET_W), F32), pltpu.VMEM((lc, RET_W), F32)],
        compiler_params=_cparams(1),
        name="retention",
    )(lg, lgl, gn_g, havg, rq, rk, rv, gate, cq, ck, cv, cgate)


def _fourier_kernel(c_ref, s_ref, flip_ref, u_ref, o_ref, *, t, blk, norm):
    half = t // 2
    n_blk = half // blk
    row0_wide = lax.broadcasted_iota(jnp.int32, (blk, 2 * FOUR_W), 0) == 0
    folded_c, folded_s = [], []
    for i in range(n_blk):
        rev = jnp.dot(flip_ref[...], u_ref[t - (i + 1) * blk:t - i * blk, :], preferred_element_type=F32)
        if i > 0:
            rev = jnp.where(row0_wide, u_ref[t - i * blk:t - i * blk + 1, :].astype(F32), rev)
        low = u_ref[i * blk:(i + 1) * blk, :].astype(F32)
        folded_c.append((low[:, 0:FOUR_W] + rev[:, 0:FOUR_W]).astype(BF16))
        folded_s.append((low[:, FOUR_W:] - rev[:, FOUR_W:]).astype(BF16))
    rows = c_ref.shape[0]
    sign = norm * (1.0 - 2.0 * (lax.broadcasted_iota(jnp.int32, (rows, 1), 0) % 2).astype(F32))
    a = (jnp.dot(c_ref[...], jnp.concatenate(folded_c, axis=0), preferred_element_type=F32)
         + sign * u_ref[half:half + 1, 0:FOUR_W].astype(F32))
    bs = jnp.dot(s_ref[...], jnp.concatenate(folded_s, axis=0), preferred_element_type=F32)
    o_ref[0:half, :] = (a[0:half] + bs[0:half]).astype(BF16)
    mid = (a[half:half + 1] + bs[half:half + 1]).astype(BF16)
    mirror = (a[0:half] - bs[0:half]).astype(BF16)
    row0 = lax.broadcasted_iota(jnp.int32, (blk, FOUR_W), 0) == 0
    for i in range(n_blk):
        src = mirror[(n_blk - 1 - i) * blk:(n_blk - i) * blk]
        rev = jnp.dot(flip_ref[...], src, preferred_element_type=F32).astype(BF16)
        first = mid if i == 0 else mirror[(n_blk - i) * blk:(n_blk - i) * blk + 1]
        o_ref[half + i * blk:half + (i + 1) * blk, :] = jnp.where(row0, first, rev)


def _fourier_call(cmat, smat_neg, flip, u):
    b, t, _ = u.shape
    rows = cmat.shape[0]
    blk = flip.shape[0]
    const = lambda *shape: pl.BlockSpec(shape, lambda i: (0,) * len(shape))
    return pl.pallas_call(
        functools.partial(_fourier_kernel, t=t, blk=blk, norm=float(t * (FOUR_W // FOURIER_GROUPS)) ** -0.5),
        grid=(b,),
        in_specs=[const(rows, t // 2), const(rows, t // 2), const(blk, blk),
                  pl.BlockSpec((None, t, 2 * FOUR_W), lambda i: (i, 0, 0))],
        out_specs=pl.BlockSpec((None, t, FOUR_W), lambda i: (i, 0, 0)),
        out_shape=jax.ShapeDtypeStruct((b, t, FOUR_W), BF16),
        compiler_params=_cparams(1),
        name="fourier_mix",
    )(cmat, smat_neg, flip, u)


def _mlp_kernel(x_ref, mod_ref, g_ref, att_ref, ret_ref, four_ref, wo_ref, w1_ref, w2_ref, o_ref, *, ff_chunk):
    m = (jnp.dot(att_ref[...], wo_ref[0:ATT_W, :], preferred_element_type=F32)
         + jnp.dot(ret_ref[...], wo_ref[ATT_W:ATT_W + RET_W, :], preferred_element_type=F32)
         + jnp.dot(four_ref[...], wo_ref[ATT_W + RET_W:, :], preferred_element_type=F32))
    x1 = x_ref[...] + mod_ref[2:3, :] * m
    ms = jnp.mean(x1 * x1, axis=-1, keepdims=True)
    hn = ((x1 * lax.rsqrt(ms + EPS)) * g_ref[...]) * (1.0 + mod_ref[4:5, :]) + mod_ref[3:4, :]
    hb = hn.astype(BF16)
    d_ff = w1_ref.shape[1]
    acc = jnp.zeros(x1.shape, F32)
    for c in range(d_ff // ff_chunk):
        hid = jnp.dot(hb, w1_ref[:, c * ff_chunk:(c + 1) * ff_chunk], preferred_element_type=F32)
        hid = jnp.square(jnp.maximum(hid, 0.0)).astype(BF16)
        acc = acc + jnp.dot(hid, w2_ref[c * ff_chunk:(c + 1) * ff_chunk, :], preferred_element_type=F32)
    o_ref[...] = x1 + mod_ref[5:6, :] * acc


def _mlp_call(layer, x, mod, mod_row, g2, att, ret, four, w_out, w1, w2, *, tm):
    b, t, d = x.shape
    d_ff = w1.shape[-1]
    tok = lambda w: pl.BlockSpec((None, tm, w), lambda i, j: (i, j, 0))
    once = dict(pipeline_mode=pl.Buffered(1))
    per_layer = lambda *shape, **kw: pl.BlockSpec((None,) + shape, lambda i, j: (layer,) + (0,) * len(shape), **kw)
    return pl.pallas_call(
        functools.partial(_mlp_kernel, ff_chunk=1024),
        grid=(b, t // tm),
        in_specs=[tok(d),
                  pl.BlockSpec((None, None, N_MOD, d), lambda i, j: (layer, mod_row(i), 0, 0)),
                  per_layer(1, d),
                  tok(ATT_W), tok(RET_W), tok(FOUR_W),
                  per_layer(d, d, **once),
                  per_layer(d, d_ff, **once),
                  per_layer(d_ff, d, **once)],
        out_specs=tok(d),
        out_shape=jax.ShapeDtypeStruct((b, t, d), F32),
        compiler_params=_cparams(2, fuse_inputs=[i >= 6 for i in range(9)]),
        name="out_projection_mlp",
    )(x, mod, g2, att, ret, four, w_out, w1, w2)


def _rope_tables(pos_groups, half):
    freqs = ROPE_BASE ** (-jnp.arange(half, dtype=F32) / half)
    cos, sin_next, sin_prev = [], [], []
    zeros = None
    for pos in pos_groups:
        ang = pos.astype(F32)[:, None] * freqs[None, :]
        c, s = jnp.cos(ang), jnp.sin(ang)
        zeros = jnp.zeros_like(s)
        cos += [c, c]
        sin_next += [-s, zeros]
        sin_prev += [zeros, s]
    reps = LANES // (2 * half * len(pos_groups))
    cat = lambda parts: jnp.tile(jnp.concatenate(parts, axis=1), (1, reps))
    return cat(cos), cat(sin_next), cat(sin_prev)


def _dft_tables(n):
    k = np.arange(n, dtype=np.int64)
    ang = 2.0 * np.pi * ((k[:, None] * k[None, :]) % n).astype(np.float64) / n
    return np.cos(ang), np.sin(ang)


def _block_diag(blocks):
    n = blocks.shape[-1]
    g = blocks.shape[-3]
    eye = jnp.eye(g, dtype=blocks.dtype)
    out = blocks[..., :, :, None, :] * eye[:, None, :, None]
    return out.reshape(blocks.shape[:-3] + (g * n, g * n))


def kernel(x, c, ctx, c_ctx, w_mod, b_mod, norm1_g, norm2_g, w_in, w_out, q_norm_g, k_norm_g, attn_sink,
           ret_decay_logit, ret_gn_g, fourier_w, w_ff1, w_ff2):
    b, s, d = x.shape
    lc = ctx.shape[1]
    depth = w_mod.shape[0]
    fdim = fourier_w.shape[-1]

    n_rows = -(-(b + 1) // 16) * 16
    cvec = jnp.zeros((n_rows, d), F32).at[:b].set(c).at[b].set(c_ctx)
    mod = _mod_call(cvec, w_mod, b_mod).reshape(depth, n_rows, N_MOD, d)

    pos = jnp.arange(s)
    tabs_lat = _rope_tables([pos // GRID_W, pos % GRID_W], HEAD_DIM // 4) + _rope_tables([pos], HEAD_DIM // 2)
    pair = 2 if b % 2 == 0 else 1
    paired = lambda v: v.reshape(b // pair, pair * lc, v.shape[-1])
    unpaired = lambda v: v.reshape(b, lc, v.shape[-1])
    tabs_ctx = tuple(jnp.zeros((pair * lc, LANES), F32) for _ in range(6))
    head_id = np.arange(256) // HEAD_DIM
    havg = jnp.asarray((head_id[:, None] == head_id[None, :]) / HEAD_DIM, BF16)
    c_ch, s_ch = _dft_tables(fdim)
    eye_g = np.eye(FOURIER_GROUPS)
    c_bd = jnp.asarray(np.kron(eye_g, c_ch), F32)
    s_bd = jnp.asarray(np.kron(eye_g, s_ch), F32)

    def pos_tables(n):
        cn, sn = _dft_tables(n)
        norm = 1.0 / np.sqrt(float(n) * fdim)
        rows = n // 2 + 16
        blk = min(256, n // 2)
        r = np.arange(blk)
        flip = ((r[:, None] + r[None, :]) == blk).astype(np.float32)
        cols = n // 2
        return (jnp.asarray(cn[:rows, :cols] * norm, F32).astype(BF16),
                jnp.asarray(-sn[:rows, :cols] * norm, F32).astype(BF16), jnp.asarray(flip, BF16))

    ftab_s = pos_tables(s)
    ftab_c = pos_tables(lc)
    win_bias = _window_bias(2 * BLOCK)
    logit_bound = (HEAD_DIM * Q_SCALE * 1.02) * jnp.max(jnp.abs(q_norm_g), axis=-1) * jnp.max(jnp.abs(k_norm_g), axis=-1)
    softmax_shift = jnp.maximum(logit_bound, jnp.max(attn_sink, axis=-1) * LOG2E)
    spread = logit_bound + softmax_shift
    attn_aux = jnp.stack([softmax_shift, (spread < 100.0).astype(F32)], axis=-1).astype(F32)

    wc, ws = _fourier_prep_call(_block_diag(fourier_w.astype(F32)), c_bd, s_bd)
    wcs = jnp.concatenate([wc, ws], axis=-1).astype(BF16)

    w_in_b = w_in.astype(BF16)
    w_out_b = w_out.astype(BF16)
    w1_b = w_ff1.astype(BF16)
    w2_b = w_ff2.astype(BF16)
    g1 = norm1_g.reshape(depth, 1, d)
    g2 = norm2_g.reshape(depth, 1, d)
    gq = jnp.tile(q_norm_g * Q_SCALE, (1, 256 // HEAD_DIM)).reshape(depth, 1, 256)
    gk = jnp.tile(k_norm_g, (1, LANES // HEAD_DIM)).reshape(depth, 1, LANES)
    log_gamma = -jax.nn.softplus(-ret_decay_logit.astype(F32))
    log_gamma_lanes = jnp.repeat(log_gamma, HEAD_DIM, axis=-1)

    lat_row = lambda i: i
    ctx_row = lambda i: b
    tm_lat = 1024 if s % 1024 == 0 else 512

    h = ctx
    for l in range(depth):
        need_ctx = l < depth - 1
        common = (g1, w_in_b, gq, gk, havg, wcs)
        q, kz, vz, rq, rk, rv, gate, u = _inproj_call(l, x, mod, lat_row, *common, tabs_lat, rope=True, tm=tm_lat)
        ctx_out = _inproj_call(l, paired(h), mod, ctx_row, *common, tabs_ctx, rope=False, tm=pair * lc)
        cq, ckz, cvz, crq, crk, crv, cgate, cu = (unpaired(o) for o in ctx_out)

        att = _attn_call(attn_sink[l], attn_aux[l], win_bias, q, kz, vz, ckz, cvz, band=True)
        ret, ret_c = _ret_call(log_gamma[l], log_gamma_lanes[l], ret_gn_g[l].reshape(1, RET_W), havg,
                               rq, rk, rv, gate, crq, crk, crv, cgate)
        four = _fourier_call(*ftab_s, u)
        x = _mlp_call(l, x, mod, lat_row, g2, att, ret, four, w_out_b, w1_b, w2_b, tm=tm_lat)
        if need_ctx:
            att_c = _attn_call(attn_sink[l], attn_aux[l], win_bias, cq, ckz, cvz, ckz, cvz, band=False)
            four_c = _fourier_call(*ftab_c, cu)
            h = unpaired(_mlp_call(l, paired(h), mod, ctx_row, g2, paired(att_c), paired(ret_c), paired(four_c),
                                   w_out_b, w1_b, w2_b, tm=pair * lc))
    return x
```
